```python
import math
import jax, jax.numpy as jnp
from jax import lax
import numpy as np

D_MODEL = 1024
BATCH = 2
SEQ = 8192
DEPTH = 4
DEC_BATCH = 32
DEC_SEQ = 8
PAST_LEN = 8192
PAGE_SIZE = 128

N_A_LAYERS = DEPTH // 2
N_B_LAYERS = DEPTH - N_A_LAYERS
A_EXPAND = 128
A_HEADS = D_MODEL // A_EXPAND
A_DK = A_EXPAND
A_DV = D_MODEL // A_HEADS
A_WIDTH = A_HEADS * A_DK
A_VWIDTH = A_HEADS * A_DV
A_CHUNK = 64
B_HEADS = 16
B_HEAD_DIM = D_MODEL // B_HEADS
B_WIDTH = B_HEADS * B_HEAD_DIM
MOBA_BLOCK = 256
MOBA_TOPK = 3
MOBA_QGROUP = 32
N_BUCKETS = 32
MAX_DISTANCE = 128
N_EXPERTS = 32
TOP_K = 4
D_FF = D_MODEL
SWIGLU_LIMIT = 7.0
SWIGLU_ALPHA = 1.702
MOE_BLOCK = 128
DN_ALPHA = (2 * DEPTH) ** 0.25
DN_BETA = (8 * DEPTH) ** -0.25
LN_EPS = 1e-5
RMS_EPS = 1e-6
NEG_INF = -1e30

kernel_name = 'yoco_hgrn2_moba_moe_decode_step'


def layer_norm(x, g, b):
    xf = x.astype(jnp.float32)
    mu = jnp.mean(xf, axis=-1, keepdims=True)
    var = jnp.mean(jnp.square(xf - mu), axis=-1, keepdims=True)
    return ((xf - mu) * lax.rsqrt(var + LN_EPS) * g.astype(jnp.float32) + b.astype(jnp.float32)).astype(x.dtype)


def t5_bucket(dist):
    dist = jnp.maximum(dist, 0)
    max_exact = N_BUCKETS // 2
    scaled = jnp.log(jnp.maximum(dist, max_exact).astype(jnp.float32) / max_exact) / math.log(MAX_DISTANCE / max_exact)
    large = jnp.minimum(max_exact + (scaled * (N_BUCKETS - max_exact)).astype(jnp.int32), N_BUCKETS - 1)
    return jnp.where(dist < max_exact, dist, large)


def hgrn2_chunked(q, k, v, log_f, s0):
    B, T, H, DK = q.shape
    DV = v.shape[-1]
    C = min(A_CHUNK, T)
    n = -(-T // C)
    pad = n * C - T

    def to_chunks(a):
        a = jnp.pad(a, ((0, 0), (0, pad), (0, 0), (0, 0)))
        return a.reshape(B, n, C, H, a.shape[-1]).transpose(1, 0, 3, 2, 4)

    qc, kc, vc, gc = to_chunks(q), to_chunks(k), to_chunks(v), to_chunks(log_f)
    causal = jnp.tril(jnp.ones((C, C), dtype=bool))[:, :, None]

    def step(S, inp):
        qb, kb, vb, gb = inp
        b = jnp.cumsum(gb, axis=2)
        o_inter = jnp.einsum('bhtk,bhkv->bhtv', qb * jnp.exp(b), S)
        diff = b[:, :, :, None, :] - b[:, :, None, :, :]
        decay = jnp.where(causal, jnp.exp(jnp.where(causal, diff, 0.0)), 0.0)
        A = jnp.einsum('bhtk,bhsk,bhtsk->bhts', qb, kb, decay)
        o_intra = jnp.einsum('bhts,bhsv->bhtv', A, vb)
        b_last = b[:, :, -1]
        S_new = jnp.exp(b_last)[..., None] * S + jnp.einsum('bhsk,bhsv->bhkv', kb * jnp.exp(b_last[:, :, None] - b), vb)
        return S_new, o_inter + o_intra

    S, o = lax.scan(step, s0, (qc, kc, vc, gc))
    o = o.transpose(1, 0, 3, 2, 4).reshape(B, n * C, H, DV)[:, :T]
    return o, S


def hgrn2_mixer(x, w_in, lower, norm_w, w_out, s0):
    B, T, _ = x.shape
    proj = (x @ w_in).astype(jnp.float32)
    q, fz, i, g = jnp.split(proj, [A_WIDTH, 2 * A_WIDTH, 2 * A_WIDTH + A_VWIDTH], axis=-1)
    f = lower + (1.0 - lower) * jax.nn.sigmoid(fz)
    log_f = jnp.log(f)
    k = 1.0 - f
    q = jax.nn.silu(q)
    o, S = hgrn2_chunked(q.reshape(B, T, A_HEADS, A_DK), k.reshape(B, T, A_HEADS, A_DK),
                         i.reshape(B, T, A_HEADS, A_DV), log_f.reshape(B, T, A_HEADS, A_DK),
                         s0.astype(jnp.float32))
    o = o.reshape(B, T, A_VWIDTH)
    o = o * lax.rsqrt(jnp.mean(jnp.square(o), axis=-1, keepdims=True) + RMS_EPS) * norm_w.astype(jnp.float32)
    o = o * jax.nn.silu(g)
    return o.astype(x.dtype) @ w_out, S.astype(s0.dtype)


def build_shared_kv(past_k, past_v, k_new, v_new):
    B, P, H, DH = past_k.shape
    L = P + k_new.shape[1]
    nB = -(-L // MOBA_BLOCK)
    z = jnp.zeros((B, nB * MOBA_BLOCK - L, H, DH), k_new.dtype)
    kb = jnp.concatenate([past_k.astype(k_new.dtype), k_new, z], axis=1).reshape(B, nB, MOBA_BLOCK, H, DH)
    vb = jnp.concatenate([past_v.astype(v_new.dtype), v_new, z], axis=1).reshape(B, nB, MOBA_BLOCK, H, DH)
    k_mean = jnp.mean(kb.astype(jnp.float32), axis=2)
    return kb, vb, k_mean


def moba_attend(q, k_blocks, v_blocks, k_mean, rel_bias, q_pos):
    B, T, H, DH = q.shape
    nB = k_blocks.shape[1]
    n_top = min(MOBA_TOPK, nB)
    QG = min(MOBA_QGROUP, T)
    nG = -(-T // QG)
    pad = nG * QG - T
    q_items = jnp.pad(q, ((0, 0), (0, pad), (0, 0), (0, 0))).reshape(B * nG, QG, H, DH)
    pos = jnp.concatenate([q_pos, jnp.full((pad,), q_pos[-1], jnp.int32)])
    pos_items = jnp.tile(pos.reshape(nG, QG), (B, 1))
    b_items = jnp.repeat(jnp.arange(B, dtype=jnp.int32), nG)
    h_idx = jnp.arange(H)[:, None, None]
    blk_off = jnp.arange(MOBA_BLOCK, dtype=jnp.int32)
    scale = DH ** -0.5
    rel_t = rel_bias.astype(jnp.float32).T

    def one_group(args):
        qi, pi, b = args
        kb, vb, km = k_blocks[b], v_blocks[b], k_mean[b]
        qf = qi.astype(jnp.float32)
        cur = pi // MOBA_BLOCK
        gate = jnp.einsum('qhd,nhd->hqn', qf, km)
        past = jnp.arange(nB)[None, None, :] < cur[None, :, None]
        gate = jnp.where(past, gate, -jnp.inf)
        top_val, top_idx = lax.top_k(gate, n_top)
        idx = jnp.concatenate([top_idx, jnp.broadcast_to(cur[None, :, None], (H, QG, 1))], axis=-1)
        ok = jnp.concatenate([top_val > -jnp.inf, jnp.ones((H, QG, 1), dtype=bool)], axis=-1)
        ks = kb[idx, :, h_idx].astype(jnp.float32)
        vs = vb[idx, :, h_idx].astype(jnp.float32)
        key_pos = idx[..., None] * MOBA_BLOCK + blk_off
        dist = pi[None, :, None, None] - key_pos
        valid = ok[..., None] & (dist >= 0)
        logits = jnp.einsum('qhd,hqjkd->hqjk', qf, ks) * scale + rel_t[h_idx[..., None], t5_bucket(dist)]
        logits = jnp.where(valid, logits, NEG_INF).reshape(H, QG, -1)
        p = jax.nn.softmax(logits, axis=-1)
        o = jnp.einsum('hqm,hqmd->qhd', p, vs.reshape(H, QG, -1, DH))
        return o.astype(q.dtype)

    out = lax.map(one_group, (q_items, pos_items, b_items))
    return out.reshape(B, nG * QG, H, DH)[:, :T]


def moba_mixer(x, w_q, w_o, k_blocks, v_blocks, k_mean, rel_bias, q_pos):
    B, T, _ = x.shape
    q = (x @ w_q).reshape(B, T, B_HEADS, B_HEAD_DIM)
    o = moba_attend(q, k_blocks, v_blocks, k_mean, rel_bias, q_pos)
    return o.reshape(B, T, B_WIDTH) @ w_o


def moe_ffn(x, router_w, router_b, w_up, b_up, w_down, b_down):
    N, D = x.shape
    logits = x.astype(jnp.float32) @ router_w.astype(jnp.float32) + router_b.astype(jnp.float32)
    top_val, top_idx = lax.top_k(logits, TOP_K)
    gates = jax.nn.softmax(top_val, axis=-1)
    NK = N * TOP_K
    flat_e = top_idx.reshape(NK)
    order = jnp.argsort(flat_e)
    sorted_e = flat_e[order]
    tok = (order // TOP_K).astype(jnp.int32)
    counts = jnp.bincount(flat_e, length=N_EXPERTS)
    padded = (counts + MOE_BLOCK - 1) // MOE_BLOCK * MOE_BLOCK
    ends = jnp.cumsum(padded)
    pad_start = ends - padded
    start = jnp.cumsum(counts) - counts
    dest = pad_start[sorted_e] + (jnp.arange(NK) - start[sorted_e])
    n_blk = -(-NK // MOE_BLOCK) + N_EXPERTS
    P = n_blk * MOE_BLOCK
    buf_tok = jnp.full((P,), N, jnp.int32).at[dest].set(tok)
    xb = jnp.concatenate([x, jnp.zeros((1, D), x.dtype)], axis=0)[buf_tok].reshape(n_blk, MOE_BLOCK, D)
    blk_e = jnp.minimum(jnp.searchsorted(ends, jnp.arange(n_blk) * MOE_BLOCK, side='right'), N_EXPERTS - 1)

    def expert_rows(args):
        xi, e = args
        h = (xi @ w_up[e] + b_up[e]).astype(jnp.float32)
        glu = jnp.minimum(h[:, 0::2], SWIGLU_LIMIT)
        lin = jnp.clip(h[:, 1::2], -SWIGLU_LIMIT, SWIGLU_LIMIT)
        a = glu * jax.nn.sigmoid(SWIGLU_ALPHA * glu) * (lin + 1.0)
        return a.astype(x.dtype) @ w_down[e] + b_down[e]

    yb = lax.map(expert_rows, (xb, blk_e)).reshape(P, D)
    y = yb[dest].astype(jnp.float32) * gates.reshape(NK)[order][:, None]
    return jax.ops.segment_sum(y, tok, num_segments=N).astype(x.dtype)


def trunk(x, s0_all, past_k, past_v, a_w_in, a_lb, a_norm, a_w_out, kv_w, b_w_q, b_w_o, rel_bias,
          router_w, router_b, w_up, b_up, w_down, b_down, ln_g, ln_b):
    B, T, D = x.shape
    P = past_k.shape[1]
    q_pos = P + jnp.arange(T, dtype=jnp.int32)
    p_lb = jax.nn.softmax(a_lb.astype(jnp.float32), axis=0)
    lower = jnp.cumsum(p_lb, axis=0) - p_lb[0]
    states = []
    for l in range(DEPTH):
        if l < N_A_LAYERS:
            h, s_new = hgrn2_mixer(x, a_w_in[l], lower[l], a_norm[l], a_w_out[l], s0_all[l])
            states.append(s_new)
        else:
            if l == N_A_LAYERS:
                kv = x @ kv_w
                k_new = kv[..., :B_WIDTH].reshape(B, T, B_HEADS, B_HEAD_DIM)
                v_new = kv[..., B_WIDTH:].reshape(B, T, B_HEADS, B_HEAD_DIM)
                k_blocks, v_blocks, k_mean = build_shared_kv(past_k, past_v, k_new, v_new)
            j = l - N_A_LAYERS
            h = moba_mixer(x, b_w_q[j], b_w_o[j], k_blocks, v_blocks, k_mean, rel_bias, q_pos)
        x = layer_norm(DN_ALPHA * x + h, ln_g[l, 0], ln_b[l, 0])
        m = moe_ffn(x.reshape(B * T, D), router_w[l], router_b[l], w_up[l], b_up[l], w_down[l], b_down[l])
        x = layer_norm(DN_ALPHA * x + m.reshape(B, T, D), ln_g[l, 1], ln_b[l, 1])
    return x, k_new, v_new, jnp.stack(states)


def setup_inputs(seed: int = 0) -> dict:
    key = jax.random.key(seed)
    k = jax.random.split(key, 22)
    n_pages = PAST_LEN // PAGE_SIZE
    n_used = DEC_BATCH * n_pages
    n_phys = (5 * n_used + 3) // 4

    def nrm(kk, shape, scale):
        return jax.random.normal(kk, shape, jnp.float32) * scale

    return {
        'x_prompt': nrm(k[0], (BATCH, SEQ, D_MODEL), 1.0),
        'x_sample': nrm(k[1], (DEC_BATCH, DEC_SEQ, D_MODEL), 1.0),
        'cache_k': nrm(k[2], (n_phys, PAGE_SIZE, B_HEADS, B_HEAD_DIM), 1.0),
        'cache_v': nrm(k[3], (n_phys, PAGE_SIZE, B_HEADS, B_HEAD_DIM), 1.0),
        'state_hgrn': nrm(k[4], (N_A_LAYERS, DEC_BATCH, A_HEADS, A_DK, A_DV), 0.5),
        'page_table': jax.random.permutation(k[5], n_phys)[:n_used].reshape(DEC_BATCH, n_pages).astype(jnp.int32),
        'a_w_in': nrm(k[6], (N_A_LAYERS, D_MODEL, 2 * A_WIDTH + 2 * A_VWIDTH), D_MODEL ** -0.5),
        'a_lb': nrm(k[7], (N_A_LAYERS, A_WIDTH), 0.5),
        'a_norm': 1.0 + nrm(k[8], (N_A_LAYERS, A_VWIDTH), 0.02),
        'a_w_out': nrm(k[9], (N_A_LAYERS, A_VWIDTH, D_MODEL), DN_BETA * A_VWIDTH ** -0.5),
        'kv_w': nrm(k[10], (D_MODEL, 2 * B_WIDTH), D_MODEL ** -0.5),
        'b_w_q': nrm(k[11], (N_B_LAYERS, D_MODEL, B_WIDTH), D_MODEL ** -0.5),
        'b_w_o': nrm(k[12], (N_B_LAYERS, B_WIDTH, D_MODEL), DN_BETA * B_WIDTH ** -0.5),
        'rel_bias': nrm(k[13], (N_BUCKETS, B_HEADS), 0.5),
        'router_w': nrm(k[14], (DEPTH, D_MODEL, N_EXPERTS), D_MODEL ** -0.5),
        'router_b': nrm(k[15], (DEPTH, N_EXPERTS), 0.01),
        'w_up': nrm(k[16], (DEPTH, N_EXPERTS, D_MODEL, 2 * D_FF), D_MODEL ** -0.5),
        'b_up': nrm(k[17], (DEPTH, N_EXPERTS, 2 * D_FF), 0.01),
        'w_down': nrm(k[18], (DEPTH, N_EXPERTS, D_FF, D_MODEL), DN_BETA * D_FF ** -0.5),
        'b_down': nrm(k[19], (DEPTH, N_EXPERTS, D_MODEL), 0.01),
        'ln_g': 1.0 + nrm(k[20], (DEPTH, 2, D_MODEL), 0.02),
        'ln_b': nrm(k[21], (DEPTH, 2, D_MODEL), 0.02),
    }


def reference(x_prompt, x_sample, cache_k, cache_v, state_hgrn, page_table, a_w_in, a_lb, a_norm, a_w_out,
              kv_w, b_w_q, b_w_o, rel_bias, router_w, router_b, w_up, b_up, w_down, b_down, ln_g, ln_b):
    weights = (a_w_in, a_lb, a_norm, a_w_out, kv_w, b_w_q, b_w_o, rel_bias,
               router_w, router_b, w_up, b_up, w_down, b_down, ln_g, ln_b)
    bp = x_prompt.shape[0]
    empty = jnp.zeros((bp, 0, B_HEADS, B_HEAD_DIM), x_prompt.dtype)
    s0_prompt = jnp.zeros((N_A_LAYERS, bp, A_HEADS, A_DK, A_DV), state_hgrn.dtype)
    y_prompt, k_prompt, v_prompt, s_prompt = trunk(x_prompt, s0_prompt, empty, empty, *weights)
    bs = x_sample.shape[0]
    n_pages = page_table.shape[1]
    past_k = cache_k[page_table].reshape(bs, n_pages * PAGE_SIZE, B_HEADS, B_HEAD_DIM)
    past_v = cache_v[page_table].reshape(bs, n_pages * PAGE_SIZE, B_HEADS, B_HEAD_DIM)
    y_sample, k_sample, v_sample, s_sample = trunk(x_sample, state_hgrn, past_k, past_v, *weights)
    return (y_prompt, y_sample, k_prompt, v_prompt, s_prompt, k_sample, v_sample, s_sample)
```

```python
import functools
import math

import jax
import jax.numpy as jnp
import numpy as np
from jax import lax
from jax.experimental import pallas as pl
from jax.experimental.pallas import tpu as pltpu

F32 = jnp.float32
BF16 = jnp.bfloat16
I32 = jnp.int32
HIGHEST = lax.Precision.HIGHEST

D_MODEL = 1024
DEPTH = 4
N_A_LAYERS = 2
A_HEADS = 8
A_DK = 128
B_HEADS = 16
B_HEAD_DIM = 64
MOBA_BLOCK = 256
MOBA_TOPK = 3
N_BUCKETS = 32
MAX_DISTANCE = 128
N_EXPERTS = 32
TOP_K = 4
SWIGLU_LIMIT = 7.0
SWIGLU_ALPHA = 1.702
DN_ALPHA = (2 * DEPTH) ** 0.25
LN_EPS = 1e-5
RMS_EPS = 1e-6
NEG_INF = -1e30

LANES = 128
VMEM_LIMIT = 56 * 1024 * 1024
MOE_ROWS = 256
COMBINE_ROWS = 128


def _params(sem, vmem=VMEM_LIMIT):
    return pltpu.CompilerParams(dimension_semantics=sem, vmem_limit_bytes=vmem)


def _row_tile(n, cap):
    best = None
    for t in range(8, min(n, cap) + 1, 8):
        if n % t == 0:
            best = t
    assert best is not None, n
    return best


def _dot_nt(a, b, precision=None):
    return lax.dot_general(a, b, (((1,), (1,)), ((), ())), precision=precision,
                           preferred_element_type=F32)


def _layer_norm(y, g, b):
    mu = jnp.mean(y, axis=-1, keepdims=True)
    yc = y - mu
    var = jnp.mean(yc * yc, axis=-1, keepdims=True)
    return yc * lax.rsqrt(var + LN_EPS) * g + b


def _proj_body(x_ref, w_ref, o_ref):
    o_ref[...] = jnp.dot(x_ref[...].astype(BF16), w_ref[...], preferred_element_type=F32)


def _proj(x, w_bf16):
    n, k = x.shape
    m = w_bf16.shape[1]
    tn = 1024
    tm = _row_tile(n, 640)
    return pl.pallas_call(
        _proj_body,
        grid=(m // tn, n // tm),
        in_specs=[pl.BlockSpec((tm, k), lambda j, i: (i, 0)),
                  pl.BlockSpec((k, tn), lambda j, i: (0, j))],
        out_specs=pl.BlockSpec((tm, tn), lambda j, i: (i, j)),
        out_shape=jax.ShapeDtypeStruct((n, m), F32),
        compiler_params=_params(("arbitrary", "arbitrary")),
        name="proj",
    )(x, w_bf16)


def _mm_res_ln_body(x_ref, a_ref, w_ref, g_ref, b_ref, o_ref):
    h = jnp.dot(a_ref[...].astype(BF16), w_ref[...], preferred_element_type=F32)
    o_ref[...] = _layer_norm(DN_ALPHA * x_ref[...] + h, g_ref[...], b_ref[...])


def _mm_res_ln(x, a, w_bf16, g, b):
    n, d = x.shape
    tm = _row_tile(n, 640)
    row = lambda i: (i, 0)
    fixed = lambda i: (0, 0)
    return pl.pallas_call(
        _mm_res_ln_body,
        grid=(n // tm,),
        in_specs=[pl.BlockSpec((tm, d), row), pl.BlockSpec((tm, d), row),
                  pl.BlockSpec((d, d), fixed), pl.BlockSpec((1, d), fixed), pl.BlockSpec((1, d), fixed)],
        out_specs=pl.BlockSpec((tm, d), row),
        out_shape=jax.ShapeDtypeStruct((n, d), F32),
        compiler_params=_params(("arbitrary",)),
        name="mm_res_ln",
    )(x, a, w_bf16, g.reshape(1, d), b.reshape(1, d))


def _router_body(x_ref, w_ref, b_ref, idx_ref, gate_ref):
    logits = jnp.dot(x_ref[...], w_ref[...], precision=HIGHEST, preferred_element_type=F32) + b_ref[...]
    tm = logits.shape[0]
    col = lax.broadcasted_iota(I32, logits.shape, 1)
    lane = lax.broadcasted_iota(I32, (tm, LANES), 1)
    vals, idxs = [], []
    cur = logits
    for _ in range(TOP_K):
        m = jnp.max(cur, axis=-1, keepdims=True)
        ix = jnp.min(jnp.where(cur == m, col, N_EXPERTS), axis=-1, keepdims=True)
        vals.append(m)
        idxs.append(ix)
        cur = jnp.where(col == ix, -jnp.inf, cur)
    es = [jnp.exp(v - vals[0]) for v in vals]
    tot = es[0] + es[1] + es[2] + es[3]
    idx_out = jnp.zeros((tm, LANES), I32)
    gate_out = jnp.zeros((tm, LANES), F32)
    for k in range(TOP_K):
        idx_out = jnp.where(lane == k, idxs[k], idx_out)
        gate_out = jnp.where(lane == k, es[k] / tot, gate_out)
    idx_ref[...] = idx_out
    gate_ref[...] = gate_out


def _router(x, w, b):
    n, d = x.shape
    tm = _row_tile(n, 640)
    row = lambda i: (i, 0)
    fixed = lambda i: (0, 0)
    return pl.pallas_call(
        _router_body,
        grid=(n // tm,),
        in_specs=[pl.BlockSpec((tm, d), row), pl.BlockSpec((d, N_EXPERTS), fixed),
                  pl.BlockSpec((1, N_EXPERTS), fixed)],
        out_specs=[pl.BlockSpec((tm, LANES), row), pl.BlockSpec((tm, LANES), row)],
        out_shape=[jax.ShapeDtypeStruct((n, LANES), I32), jax.ShapeDtypeStruct((n, LANES), F32)],
        compiler_params=_params(("arbitrary",)),
        name="router",
    )(x, w, b.reshape(1, N_EXPERTS))


def _expert_body(blk_e_ref, buf_tok_ref, nblk_ref,
                 x_hbm, wg_ref, wl_ref, bg_ref, bl_ref, wd_ref, bd_ref, y_ref, xbuf, sem):
    del blk_e_ref
    i = pl.program_id(0)
    nb = nblk_ref[0]
    slot = lax.rem(i, 2)

    def row_copy(tok, slot_, r):
        return pltpu.make_async_copy(x_hbm.at[pl.ds(tok, 1)], xbuf.at[slot_, pl.ds(r, 1)], sem.at[slot_])

    def issue(blk, slot_):
        base = blk * MOE_ROWS

        def body(r, c):
            row_copy(buf_tok_ref[base + r], slot_, r).start()
            return c
        lax.fori_loop(0, MOE_ROWS, body, 0)

    def wait(slot_):
        def body(r, c):
            row_copy(0, slot_, r).wait()
            return c
        lax.fori_loop(0, MOE_ROWS, body, 0)

    @pl.when(i == 0)
    def _():
        issue(0, 0)

    @pl.when(i + 1 < nb)
    def _():
        issue(i + 1, 1 - slot)

    @pl.when(i < nb)
    def _():
        wait(slot)
        xb = xbuf[slot].astype(BF16)
        hg = jnp.dot(xb, wg_ref[0], preferred_element_type=F32) + bg_ref[0]
        hl = jnp.dot(xb, wl_ref[0], preferred_element_type=F32) + bl_ref[0]
        glu = jnp.minimum(hg, SWIGLU_LIMIT)
        lin = jnp.clip(hl, -SWIGLU_LIMIT, SWIGLU_LIMIT)
        a = glu * jax.nn.sigmoid(SWIGLU_ALPHA * glu) * (lin + 1.0)
        y_ref[...] = jnp.dot(a.astype(BF16), wd_ref[0], preferred_element_type=F32) + bd_ref[0]

    @pl.when(i >= nb)
    def _():
        y_ref[...] = jnp.zeros(y_ref.shape, F32)


def _experts(x, blk_e, buf_tok, n_used, wg, wl, bg, bl, wd, bd):
    n, d = x.shape
    n_blk = blk_e.shape[0]
    w_map = lambda i, be, bt, nb: (be[i], 0, 0)
    grid_spec = pltpu.PrefetchScalarGridSpec(
        num_scalar_prefetch=3,
        grid=(n_blk,),
        in_specs=[pl.BlockSpec(memory_space=pl.ANY),
                  pl.BlockSpec((1, d, d), w_map), pl.BlockSpec((1, d, d), w_map),
                  pl.BlockSpec((1, 1, d), w_map), pl.BlockSpec((1, 1, d), w_map),
                  pl.BlockSpec((1, d, d), w_map), pl.BlockSpec((1, 1, d), w_map)],
        out_specs=pl.BlockSpec((MOE_ROWS, d), lambda i, be, bt, nb: (i, 0)),
        scratch_shapes=[pltpu.VMEM((2, MOE_ROWS, d), F32), pltpu.SemaphoreType.DMA((2,))],
    )
    return pl.pallas_call(
        _expert_body,
        grid_spec=grid_spec,
        out_shape=jax.ShapeDtypeStruct((n_blk * MOE_ROWS, d), F32),
        compiler_params=_params(("arbitrary",)),
        name="experts",
    )(blk_e, buf_tok, n_used, x, wg, wl, bg, bl, wd, bd)


def _combine_body(pos_ref, x_ref, gate_ref, y_hbm, g_ref, b_ref, o_ref, ybuf, sem):
    i = pl.program_id(0)
    nsteps = pl.num_programs(0)
    slot = lax.rem(i, 2)
    tm = COMBINE_ROWS

    def row_copy(src, slot_, k, r):
        return pltpu.make_async_copy(y_hbm.at[pl.ds(src, 1)], ybuf.at[slot_, k, pl.ds(r, 1)], sem.at[slot_])

    def issue(step, slot_):
        base = step * (tm * TOP_K)

        def body(r, c):
            for k in range(TOP_K):
                row_copy(pos_ref[base + r * TOP_K + k], slot_, k, r).start()
            return c
        lax.fori_loop(0, tm, body, 0)

    def wait(slot_):
        def body(r, c):
            for k in range(TOP_K):
                row_copy(0, slot_, k, r).wait()
            return c
        lax.fori_loop(0, tm, body, 0)

    @pl.when(i == 0)
    def _():
        issue(0, 0)

    @pl.when(i + 1 < nsteps)
    def _():
        issue(i + 1, 1 - slot)

    wait(slot)
    gates = gate_ref[...]
    m = gates[:, 0:1] * ybuf[slot, 0]
    for k in range(1, TOP_K):
        m = m + gates[:, k:k + 1] * ybuf[slot, k]
    o_ref[...] = _layer_norm(DN_ALPHA * x_ref[...] + m, g_ref[...], b_ref[...])


def _combine(x, gates, pos_flat, yb, g, b):
    n, d = x.shape
    tm = COMBINE_ROWS
    row = lambda i, p: (i, 0)
    fixed = lambda i, p: (0, 0)
    grid_spec = pltpu.PrefetchScalarGridSpec(
        num_scalar_prefetch=1,
        grid=(n // tm,),
        in_specs=[pl.BlockSpec((tm, d), row), pl.BlockSpec((tm, LANES), row),
                  pl.BlockSpec(memory_space=pl.ANY),
                  pl.BlockSpec((1, d), fixed), pl.BlockSpec((1, d), fixed)],
        out_specs=pl.BlockSpec((tm, d), row),
        scratch_shapes=[pltpu.VMEM((2, TOP_K, tm, d), F32), pltpu.SemaphoreType.DMA((2,))],
    )
    return pl.pallas_call(
        _combine_body,
        grid_spec=grid_spec,
        out_shape=jax.ShapeDtypeStruct((n, d), F32),
        compiler_params=_params(("arbitrary",)),
        name="combine",
    )(pos_flat, x, gates, yb, g.reshape(1, d), b.reshape(1, d))


def _moe(x, router_w, router_b, wg, wl, bg, bl, wd, bd, ln_g, ln_b):
    n, d = x.shape
    idx_pad, gates = _router(x, router_w, router_b)
    nk = n * TOP_K
    flat_e = idx_pad[:, :TOP_K].reshape(nk)
    order = jnp.argsort(flat_e)
    sorted_e = flat_e[order]
    tok = (order // TOP_K).astype(I32)
    counts = jnp.bincount(flat_e, length=N_EXPERTS)
    padded = (counts + MOE_ROWS - 1) // MOE_ROWS * MOE_ROWS
    ends = jnp.cumsum(padded)
    pad_start = ends - padded
    start = jnp.cumsum(counts) - counts
    dest = (pad_start[sorted_e] + (jnp.arange(nk) - start[sorted_e])).astype(I32)
    n_blk = -(-nk // MOE_ROWS) + N_EXPERTS
    buf_tok = jnp.zeros((n_blk * MOE_ROWS,), I32).at[dest].set(tok)
    blk_e = jnp.minimum(jnp.searchsorted(ends, jnp.arange(n_blk) * MOE_ROWS, side='right'),
                        N_EXPERTS - 1).astype(I32)
    n_used = (ends[-1:] // MOE_ROWS).astype(I32)
    pos_flat = jnp.zeros((nk,), I32).at[order].set(dest)
    yb = _experts(x, blk_e, buf_tok, n_used, wg, wl, bg, bl, wd, bd)
    return _combine(x, gates, pos_flat, yb, ln_g, ln_b)


def _hgrn_cumsum_matrix(c):
    n_lv = int(math.log2(c))
    assert 1 << n_lv == c
    mats = []
    t = np.arange(c)
    for lv in range(n_lv):
        m = 1 << lv
        mat = np.zeros((c, c), np.float32)
        seg0 = (t // m) * m
        for r in range(c):
            if (r // m) % 2 == 1:
                mat[r, seg0[r]:r + 1] = 1.0
            else:
                mat[r, r + 1:seg0[r] + m] = 1.0
        mats.append(mat)
    mats.append(np.tril(np.ones((c, c), np.float32)))
    mats.append(np.triu(np.ones((c, c), np.float32), 1))
    return np.concatenate(mats, axis=0), n_lv


def _hgrn_body(c, n_lv, q_ref, f_ref, i_ref, g_ref, s0_ref, low_ref, nw_ref, l_ref,
               o_ref, sout_ref, st_scr, o_scr):
    step = pl.program_id(1)
    n_steps = pl.num_programs(1)

    @pl.when(step == 0)
    def _():
        for h in range(A_HEADS):
            st_scr[h] = s0_ref[0, h].T

    low = low_ref[...]
    f = low + (1.0 - low) * jax.nn.sigmoid(f_ref[0])
    lg = jnp.log(f)
    kk = 1.0 - f
    qz = q_ref[0]
    qq = qz * jax.nn.sigmoid(qz)
    vv = i_ref[0]

    hi = lg.astype(BF16)
    r1 = lg - hi.astype(F32)
    mid = r1.astype(BF16)
    lo = (r1 - mid.astype(F32)).astype(BF16)
    lmat = l_ref[...]
    z = (jnp.dot(lmat, hi, preferred_element_type=F32) + jnp.dot(lmat, mid, preferred_element_type=F32)
         + jnp.dot(lmat, lo, preferred_element_type=F32))

    row = lax.broadcasted_iota(I32, (c, 1), 0)
    rt = lax.broadcasted_iota(I32, (c, c), 0)
    rs = lax.broadcasted_iota(I32, (c, c), 1)
    qps, kps, same = [qq.astype(BF16)], [kk.astype(BF16)], [rt == rs]
    for lv in range(n_lv):
        e = jnp.exp(z[lv * c:(lv + 1) * c])
        upper = ((row >> lv) & 1) == 1
        qps.append(jnp.where(upper, qq * e, 0.0).astype(BF16))
        kps.append(jnp.where(upper, 0.0, kk * e).astype(BF16))
        same.append((rt >> (lv + 1)) == (rs >> (lv + 1)))
    b = z[n_lv * c:(n_lv + 1) * c]
    qb = (qq * jnp.exp(b)).astype(BF16)
    kc = (kk * jnp.exp(z[(n_lv + 1) * c:(n_lv + 2) * c])).astype(BF16)
    e_last = jnp.exp(b[c - 1:c, :])
    vb = vv.astype(BF16)

    for h in range(A_HEADS):
        sl = slice(h * A_DK, (h + 1) * A_DK)
        a = jnp.zeros((c, c), F32)
        for qp, kp, sm in zip(qps, kps, same):
            a = a + jnp.where(sm, _dot_nt(qp[:, sl], kp[:, sl]), 0.0)
        st = st_scr[h]
        o_h = _dot_nt(qb[:, sl], st.astype(BF16)) + jnp.dot(a.astype(BF16), vb[:, sl],
                                                             preferred_element_type=F32)
        o_scr[:, sl] = o_h
        upd = lax.dot_general(vb[:, sl], kc[:, sl], (((0,), (0,)), ((), ())), preferred_element_type=F32)
        st_scr[h] = st * e_last[:, sl] + upd

    o = o_scr[...]
    o = o * lax.rsqrt(jnp.mean(o * o, axis=-1, keepdims=True) + RMS_EPS) * nw_ref[...]
    gz = g_ref[0]
    o_ref[0] = o * (gz * jax.nn.sigmoid(gz))

    @pl.when(step == n_steps - 1)
    def _():
        for h in range(A_HEADS):
            sout_ref[0, h] = st_scr[h].T


def _hgrn(proj, s0, lower, norm_w, c):
    bsz, t, _ = proj.shape
    d = D_MODEL
    lmat, n_lv = _hgrn_cumsum_matrix(c)
    lmat = jnp.asarray(lmat, BF16)
    blk = lambda col: pl.BlockSpec((1, c, d), lambda b, s, col=col: (b, s, col))
    st_spec = pl.BlockSpec((1, A_HEADS, A_DK, A_DK), lambda b, s: (b, 0, 0, 0))
    vec = pl.BlockSpec((1, d), lambda b, s: (0, 0))
    return pl.pallas_call(
        functools.partial(_hgrn_body, c, n_lv),
        grid=(bsz, t // c),
        in_specs=[blk(0), blk(1), blk(2), blk(3), st_spec, vec, vec,
                  pl.BlockSpec(lmat.shape, lambda b, s: (0, 0))],
        out_specs=[pl.BlockSpec((1, c, d), lambda b, s: (b, s, 0)), st_spec],
        out_shape=[jax.ShapeDtypeStruct((bsz, t, d), F32), jax.ShapeDtypeStruct(s0.shape, F32)],
        scratch_shapes=[pltpu.VMEM((A_HEADS, A_DK, A_DK), F32), pltpu.VMEM((c, d), F32)],
        compiler_params=_params(("arbitrary", "arbitrary")),
        name=f"hgrn_c{c}",
    )(proj, proj, proj, proj, s0, lower.reshape(1, d), norm_w.reshape(1, d), lmat)


def _bias_by_distance(rel_bias, n):
    dist = jnp.arange(n, dtype=I32)
    max_exact = N_BUCKETS // 2
    scaled = jnp.log(jnp.maximum(dist, max_exact).astype(F32) / max_exact) / math.log(MAX_DISTANCE / max_exact)
    large = jnp.minimum(max_exact + (scaled * (N_BUCKETS - max_exact)).astype(I32), N_BUCKETS - 1)
    bucket = jnp.where(dist < max_exact, dist, large)
    return rel_bias.astype(F32)[bucket].T


def _top3_rows(g, rowb, n_rows):
    sel = jnp.zeros(g.shape, F32)
    for _ in range(MOBA_TOPK):
        m = jnp.max(g, axis=0, keepdims=True)
        ix = jnp.min(jnp.where(g == m, rowb, n_rows), axis=0, keepdims=True)
        pick = rowb == ix
        sel = jnp.where(pick & (m > -jnp.inf), 1.0, sel)
        g = jnp.where(pick, -jnp.inf, g)
    return sel


def _moba_p_body(n_blk, farb_ref, q_ref, k_ref, vt_ref, bd_ref, bp_ref, o_ref,
                 km_scr, sel_scr, m_scr, l_scr, acc_scr):
    hp = pl.program_id(1)
    i = pl.program_id(2)
    blk = MOBA_BLOCK

    @pl.when(i == 0)
    def _():
        def body(n, c):
            kb = k_ref[0, pl.ds(pl.multiple_of(n * blk, blk), blk), :]
            km_scr[pl.ds(n, 1), :] = jnp.mean(kb, axis=0, keepdims=True)
            return c
        lax.fori_loop(0, n_blk, body, 0)

    q = q_ref[0] * (B_HEAD_DIM ** -0.5)
    lane = lax.broadcasted_iota(I32, q.shape, 1)
    qh = [jnp.where(lane < B_HEAD_DIM, q, 0.0), jnp.where(lane >= B_HEAD_DIM, q, 0.0)]
    qhb = [x.astype(BF16) for x in qh]
    km = km_scr[...]
    rowb = lax.broadcasted_iota(I32, (n_blk, blk), 0)
    for a in range(2):
        g = _dot_nt(km, qh[a], precision=HIGHEST)
        g = jnp.where(rowb < i, g, -jnp.inf)
        sel_scr[a] = _top3_rows(g, rowb, n_blk)
        m_scr[a] = jnp.full((1, blk), NEG_INF, F32)
        l_scr[a] = jnp.zeros((1, blk), F32)
        acc_scr[a] = jnp.zeros((LANES, blk), F32)

    def attend(j, logits_fn):
        kb = k_ref[0, pl.ds(pl.multiple_of(j * blk, blk), blk), :].astype(BF16)
        vtb = vt_ref[0, j].astype(BF16)
        for a in range(2):
            s = logits_fn(a, _dot_nt(kb, qhb[a]))
            m_old = m_scr[a]
            m_new = jnp.maximum(m_old, jnp.max(s, axis=0, keepdims=True))
            alpha = jnp.exp(m_old - m_new)
            p = jnp.exp(s - m_new)
            l_scr[a] = alpha * l_scr[a] + jnp.sum(p, axis=0, keepdims=True)
            acc_scr[a] = alpha * acc_scr[a] + jnp.dot(vtb, p.astype(BF16), preferred_element_type=F32)
            m_scr[a] = m_new

    kr = lax.broadcasted_iota(I32, (blk, blk), 0)
    qc = lax.broadcasted_iota(I32, (blk, blk), 1)
    attend(i, lambda a, s: jnp.where(kr <= qc, s + bd_ref[a], NEG_INF))

    @pl.when(i >= 1)
    def _():
        attend(i - 1, lambda a, s: jnp.where(sel_scr[a, pl.ds(i - 1, 1), :] > 0.0, s + bp_ref[a], NEG_INF))

    def far(j, c):
        attend(j, lambda a, s: jnp.where(sel_scr[a, pl.ds(j, 1), :] > 0.0, s + farb_ref[2 * hp + a], NEG_INF))
        return c
    lax.fori_loop(0, jnp.maximum(i - 1, 0), far, 0)

    out0 = acc_scr[0] / l_scr[0]
    out1 = acc_scr[1] / l_scr[1]
    sub = lax.broadcasted_iota(I32, (LANES, blk), 0)
    o_ref[0] = jnp.where(sub < B_HEAD_DIM, out0, out1).T


def _moba_prompt(q, kv, bias_tbl):
    bsz, t, d = q.shape
    blk = MOBA_BLOCK
    n_blk = t // blk
    n_hp = B_HEADS // 2
    vt = kv[..., d:].reshape(bsz, n_blk, blk, d).transpose(0, 1, 3, 2)
    dd = jnp.arange(blk)[None, :] - jnp.arange(blk)[:, None]
    bias_diag = bias_tbl[:, jnp.maximum(dd, 0)]
    bias_prev = bias_tbl[:, dd + blk]
    far_bias = bias_tbl[:, 2 * blk - 1]
    return pl.pallas_call(
        functools.partial(_moba_p_body, n_blk),
        grid=(bsz, n_hp, n_blk),
        in_specs=[pl.BlockSpec(memory_space=pltpu.SMEM),
                  pl.BlockSpec((1, blk, LANES), lambda b, h, i: (b, i, h)),
                  pl.BlockSpec((1, t, LANES), lambda b, h, i: (b, 0, h)),
                  pl.BlockSpec((1, n_blk, LANES, blk), lambda b, h, i: (b, 0, h, 0)),
                  pl.BlockSpec((2, blk, blk), lambda b, h, i: (h, 0, 0)),
                  pl.BlockSpec((2, blk, blk), lambda b, h, i: (h, 0, 0))],
        out_specs=pl.BlockSpec((1, blk, LANES), lambda b, h, i: (b, i, h)),
        out_shape=jax.ShapeDtypeStruct((bsz, t, d), F32),
        scratch_shapes=[pltpu.VMEM((n_blk, LANES), F32), pltpu.VMEM((2, n_blk, blk), F32),
                        pltpu.VMEM((2, 1, blk), F32), pltpu.VMEM((2, 1, blk), F32),
                        pltpu.VMEM((2, LANES, blk), F32)],
        compiler_params=_params(("arbitrary", "arbitrary", "arbitrary")),
        name="moba_prompt",
    )(far_bias, q, kv, vt, bias_diag, bias_prev)


def _moba_s_body(n_pages, t_new, pt_ref, qbd_ref, kc_ref, vc_ref, kn_ref, vn_ref,
                 blast_ref, bnew_ref, farb_ref, o_ref,
                 s_scr, km_scr, sel_scr, pn_scr, vn_scr, acc_scr, qb_scr):
    del pt_ref
    p = pl.program_id(1)
    page = kc_ref.shape[1]
    pages_per_blk = MOBA_BLOCK // page
    n_blk = n_pages // pages_per_blk

    @pl.when(p == 0)
    def _():
        qb_scr[...] = qbd_ref[0].astype(BF16)
        km_scr[...] = jnp.zeros(km_scr.shape, F32)

    @pl.when(p < n_pages)
    def _():
        kp = kc_ref[0]
        s_scr[p] = jnp.dot(kp.astype(BF16), qb_scr[...], preferred_element_type=F32)
        row = pl.ds(p // pages_per_blk, 1)
        km_scr[row, :] = km_scr[row, :] + jnp.sum(kp, axis=0, keepdims=True)

    @pl.when(p == n_pages - 1)
    def _():
        km = km_scr[...] * (1.0 / MOBA_BLOCK)
        g = jnp.dot(km, qbd_ref[0], precision=HIGHEST, preferred_element_type=F32)
        rowb = lax.broadcasted_iota(I32, g.shape, 0)
        sel_scr[...] = _top3_rows(g, rowb, n_blk)

        sn = jnp.dot(kn_ref[0].astype(BF16), qb_scr[...], preferred_element_type=F32) + bnew_ref[...]
        kj = lax.broadcasted_iota(I32, sn.shape, 0)
        qi = lax.rem(lax.broadcasted_iota(I32, sn.shape, 1), t_new)
        sn = jnp.where(kj <= qi, sn, NEG_INF)
        farb = farb_ref[...]

        def pass_max(pg, m):
            bias = jnp.where(pg == n_pages - 1, blast_ref[...], farb)
            keep = sel_scr[pl.ds(pg // pages_per_blk, 1), :] > 0.0
            s = jnp.where(keep, s_scr[pg] + bias, NEG_INF)
            s_scr[pg] = s
            return jnp.maximum(m, jnp.max(s, axis=0, keepdims=True))
        m = lax.fori_loop(0, n_pages, pass_max, jnp.max(sn, axis=0, keepdims=True))

        def pass_exp(pg, l):
            e = jnp.exp(s_scr[pg] - m)
            s_scr[pg] = e
            return l + jnp.sum(e, axis=0, keepdims=True)
        en = jnp.exp(sn - m)
        l = lax.fori_loop(0, n_pages, pass_exp, jnp.sum(en, axis=0, keepdims=True))
        inv = 1.0 / l

        def pass_norm(pg, c):
            s_scr[pg] = s_scr[pg] * inv
            return c
        lax.fori_loop(0, n_pages, pass_norm, 0)
        pn_scr[...] = jnp.zeros(pn_scr.shape, F32)
        pn_scr[0:t_new, :] = en * inv
        vn_scr[...] = jnp.zeros(vn_scr.shape, F32)
        vn_scr[0:t_new, :] = vn_ref[0]
        acc_scr[...] = jnp.zeros(acc_scr.shape, F32)

    @pl.when(p >= n_pages)
    def _():
        pt = s_scr[p - n_pages].T.astype(BF16)
        acc_scr[...] = acc_scr[...] + jnp.dot(pt, vc_ref[0].astype(BF16), preferred_element_type=F32)

    @pl.when(p == 2 * n_pages - 1)
    def _():
        acc = acc_scr[...] + jnp.dot(pn_scr[...].T.astype(BF16), vn_scr[...].astype(BF16),
                                     preferred_element_type=F32)
        rh = lax.broadcasted_iota(I32, acc.shape, 0) // t_new
        ch = lax.broadcasted_iota(I32, acc.shape, 1) // B_HEAD_DIM
        acc = jnp.where(rh == ch, acc, 0.0)
        o_ref[0] = jnp.sum(acc.reshape(B_HEADS, t_new, D_MODEL), axis=0)


def _moba_sample(q, kv, cache_k, cache_v, page_table, bias_tbl):
    s, t_new, d = q.shape
    n_pages = page_table.shape[1]
    page = cache_k.shape[1]
    assert B_HEADS * t_new == LANES and page == LANES and MOBA_BLOCK % page == 0
    n_blk = n_pages // (MOBA_BLOCK // page)
    scale = B_HEAD_DIM ** -0.5
    qh = (q * scale).reshape(s, t_new, B_HEADS, B_HEAD_DIM).transpose(0, 2, 3, 1)
    eye = jnp.eye(B_HEADS, dtype=F32)
    qbd = (qh[:, :, :, None, :] * eye[None, :, None, :, None]).reshape(s, d, LANES)
    hq_h = jnp.arange(LANES) // t_new
    hq_i = jnp.arange(LANES) % t_new
    keys = jnp.arange(page)
    bias_last = bias_tbl[hq_h[None, :], (page + hq_i[None, :] - keys[:, None])]
    kj = jnp.arange(t_new)
    bias_new = bias_tbl[hq_h[None, :], jnp.maximum(hq_i[None, :] - kj[:, None], 0)]
    far_bias = bias_tbl[hq_h, 2 * MOBA_BLOCK - 1].reshape(1, LANES)
    k_new = kv[..., :d]
    v_new = kv[..., d:]
    kmap = lambda b, p, pt: (pt[b * n_pages + jnp.minimum(p, n_pages - 1)], 0, 0)
    vmap = lambda b, p, pt: (pt[b * n_pages + jnp.maximum(p - n_pages, 0)], 0, 0)
    seq = lambda b, p, pt: (b, 0, 0)
    fixed = lambda b, p, pt: (0, 0)
    grid_spec = pltpu.PrefetchScalarGridSpec(
        num_scalar_prefetch=1,
        grid=(s, 2 * n_pages),
        in_specs=[pl.BlockSpec((1, d, LANES), seq),
                  pl.BlockSpec((1, page, d), kmap), pl.BlockSpec((1, page, d), vmap),
                  pl.BlockSpec((1, t_new, d), seq), pl.BlockSpec((1, t_new, d), seq),
                  pl.BlockSpec((page, LANES), fixed), pl.BlockSpec((t_new, LANES), fixed),
                  pl.BlockSpec((1, LANES), fixed)],
        out_specs=pl.BlockSpec((1, t_new, d), seq),
        scratch_shapes=[pltpu.VMEM((n_pages, page, LANES), F32), pltpu.VMEM((n_blk, d), F32),
                        pltpu.VMEM((n_blk, LANES), F32), pltpu.VMEM((page, LANES), F32),
                        pltpu.VMEM((page, d), F32), pltpu.VMEM((LANES, d), F32),
                        pltpu.VMEM((d, LANES), BF16)],
    )
    return pl.pallas_call(
        functools.partial(_moba_s_body, n_pages, t_new),
        grid_spec=grid_spec,
        out_shape=jax.ShapeDtypeStruct((s, t_new, d), F32),
        compiler_params=_params(("arbitrary", "arbitrary")),
        name="moba_sample",
    )(page_table.reshape(-1).astype(I32), qbd, cache_k, cache_v, k_new, v_new, bias_last, bias_new, far_bias)


def kernel(x_prompt, x_sample, cache_k, cache_v, state_hgrn, page_table, a_w_in, a_lb, a_norm, a_w_out,
           kv_w, b_w_q, b_w_o, rel_bias, router_w, router_b, w_up, b_up, w_down, b_down, ln_g, ln_b):
    bp, tp, d = x_prompt.shape
    bs, ts, _ = x_sample.shape
    n_p = bp * tp
    n_s = bs * ts
    x = jnp.concatenate([x_prompt.reshape(n_p, d), x_sample.reshape(n_s, d)], axis=0)

    p_lb = jax.nn.softmax(a_lb.astype(F32), axis=0)
    lower = jnp.cumsum(p_lb, axis=0) - p_lb[0]
    bias_tbl = _bias_by_distance(rel_bias, 2 * MOBA_BLOCK)
    n_phys, page = cache_k.shape[:2]
    ck = cache_k.reshape(n_phys, page, d)
    cv = cache_v.reshape(n_phys, page, d)
    zero_state = jnp.zeros((bp,) + state_hgrn.shape[2:], F32)
    hgrn_chunk = 64 if tp % 64 == 0 else tp

    states_p, states_s = [], []
    kv = None
    for l in range(DEPTH):
        if l < N_A_LAYERS:
            proj = _proj(x, a_w_in[l].astype(BF16))
            o_p, s_p = _hgrn(proj[:n_p].reshape(bp, tp, 4 * d), zero_state, lower[l], a_norm[l], hgrn_chunk)
            o_s, s_s = _hgrn(proj[n_p:].reshape(bs, ts, 4 * d), state_hgrn[l], lower[l], a_norm[l], ts)
            states_p.append(s_p)
            states_s.append(s_s)
            mix = jnp.concatenate([o_p.reshape(n_p, d), o_s.reshape(n_s, d)], axis=0)
            x = _mm_res_ln(x, mix, a_w_out[l].astype(BF16), ln_g[l, 0], ln_b[l, 0])
        else:
            j = l - N_A_LAYERS
            if kv is None:
                kv = _proj(x, kv_w.astype(BF16))
                kv_p = kv[:n_p].reshape(bp, tp, 2 * d)
                kv_s = kv[n_p:].reshape(bs, ts, 2 * d)
            q = _proj(x, b_w_q[j].astype(BF16))
            o_p = _moba_prompt(q[:n_p].reshape(bp, tp, d), kv_p, bias_tbl)
            o_s = _moba_sample(q[n_p:].reshape(bs, ts, d), kv_s, ck, cv, page_table, bias_tbl)
            mix = jnp.concatenate([o_p.reshape(n_p, d), o_s.reshape(n_s, d)], axis=0)
            x = _mm_res_ln(x, mix, b_w_o[j].astype(BF16), ln_g[l, 0], ln_b[l, 0])
        wg = w_up[l, :, :, 0::2].astype(BF16)
        wl = w_up[l, :, :, 1::2].astype(BF16)
        bg = b_up[l, :, 0::2].reshape(N_EXPERTS, 1, d)
        bl = b_up[l, :, 1::2].reshape(N_EXPERTS, 1, d)
        x = _moe(x, router_w[l], router_b[l], wg, wl, bg, bl, w_down[l].astype(BF16),
                 b_down[l].reshape(N_EXPERTS, 1, d), ln_g[l, 1], ln_b[l, 1])

    y_p = x[:n_p].reshape(bp, tp, d)
    y_s = x[n_p:].reshape(bs, ts, d)
    heads = (B_HEADS, B_HEAD_DIM)
    return (y_p, y_s,
            kv_p[..., :d].reshape(bp, tp, *heads), kv_p[..., d:].reshape(bp, tp, *heads),
            jnp.stack(states_p),
            kv_s[..., :d].reshape(bs, ts, *heads), kv_s[..., d:].reshape(bs, ts, *heads),
            jnp.stack(states_s))
```

```python
import functools
import math

import jax
import jax.numpy as jnp
import numpy as np
from jax import lax
from jax.experimental import pallas as pl
from jax.experimental.pallas import tpu as pltpu

F32 = jnp.float32
BF16 = jnp.bfloat16
I32 = jnp.int32
HIGHEST = lax.Precision.HIGHEST

D_MODEL = 1024
DEPTH = 4
N_A_LAYERS = 2
A_HEADS = 8
A_DK = 128
B_HEADS = 16
B_HEAD_DIM = 64
MOBA_BLOCK = 256
MOBA_TOPK = 3
N_BUCKETS = 32
MAX_DISTANCE = 128
N_EXPERTS = 32
TOP_K = 4
SWIGLU_LIMIT = 7.0
SWIGLU_ALPHA = 1.702
DN_ALPHA = (2 * DEPTH) ** 0.25
LN_EPS = 1e-5
RMS_EPS = 1e-6
NEG_INF = -1e30
LOG2E = math.log2(math.e)

LANES = 128
VMEM_LIMIT = 56 * 1024 * 1024
MOE_ROWS = 256
COMBINE_ROWS = 128
DISPATCH_ROWS = 128
SAMPLE_PAGES_PER_STEP = 4


def _params(sem, vmem=VMEM_LIMIT):
    return pltpu.CompilerParams(dimension_semantics=sem, vmem_limit_bytes=vmem)


def _row_tile(n, cap):
    best = None
    for t in range(8, min(n, cap) + 1, 8):
        if n % t == 0:
            best = t
    assert best is not None, n
    return best


def _dot_nt(a, b, precision=None):
    return lax.dot_general(a, b, (((1,), (1,)), ((), ())), precision=precision,
                           preferred_element_type=F32)


def _layer_norm(y, g, b):
    mu = jnp.mean(y, axis=-1, keepdims=True)
    yc = y - mu
    var = jnp.mean(yc * yc, axis=-1, keepdims=True)
    return yc * lax.rsqrt(var + LN_EPS) * g + b


def _proj_body(x_ref, w_ref, o_ref):
    o_ref[...] = jnp.dot(x_ref[...].astype(BF16), w_ref[...], preferred_element_type=F32)


def _proj(x, w_bf16):
    n, k = x.shape
    m = w_bf16.shape[1]
    tn = 1024
    tm = _row_tile(n, 640)
    return pl.pallas_call(
        _proj_body,
        grid=(m // tn, n // tm),
        in_specs=[pl.BlockSpec((tm, k), lambda j, i: (i, 0)),
                  pl.BlockSpec((k, tn), lambda j, i: (0, j))],
        out_specs=pl.BlockSpec((tm, tn), lambda j, i: (i, j)),
        out_shape=jax.ShapeDtypeStruct((n, m), F32),
        compiler_params=_params(("arbitrary", "arbitrary")),
        name="proj",
    )(x, w_bf16)


def _mm_res_ln_body(x_ref, a_ref, w_ref, g_ref, b_ref, o_ref):
    h = jnp.dot(a_ref[...].astype(BF16), w_ref[...], preferred_element_type=F32)
    o_ref[...] = _layer_norm(DN_ALPHA * x_ref[...] + h, g_ref[...], b_ref[...])


def _mm_res_ln(x, a, w_bf16, g, b):
    n, d = x.shape
    tm = _row_tile(n, 640)
    row = lambda i: (i, 0)
    fixed = lambda i: (0, 0)
    return pl.pallas_call(
        _mm_res_ln_body,
        grid=(n // tm,),
        in_specs=[pl.BlockSpec((tm, d), row), pl.BlockSpec((tm, d), row),
                  pl.BlockSpec((d, d), fixed), pl.BlockSpec((1, d), fixed), pl.BlockSpec((1, d), fixed)],
        out_specs=pl.BlockSpec((tm, d), row),
        out_shape=jax.ShapeDtypeStruct((n, d), F32),
        compiler_params=_params(("arbitrary",)),
        name="mm_res_ln",
    )(x, a, w_bf16, g.reshape(1, d), b.reshape(1, d))


def _router_body(x_ref, w_ref, b_ref, tri_ref, idx_ref, gate_ref, rank_ref, cnt_ref, carry_scr):
    @pl.when(pl.program_id(0) == 0)
    def _():
        carry_scr[...] = jnp.zeros(carry_scr.shape, F32)

    logits = jnp.dot(x_ref[...], w_ref[...], precision=HIGHEST, preferred_element_type=F32) + b_ref[...]
    tm = logits.shape[0]
    col = lax.broadcasted_iota(I32, logits.shape, 1)
    lane = lax.broadcasted_iota(I32, (tm, LANES), 1)
    vals, idxs = [], []
    cur = logits
    for _ in range(TOP_K):
        m = jnp.max(cur, axis=-1, keepdims=True)
        ix = jnp.min(jnp.where(cur == m, col, N_EXPERTS), axis=-1, keepdims=True)
        vals.append(m)
        idxs.append(ix)
        cur = jnp.where(col == ix, -jnp.inf, cur)
    es = [jnp.exp(v - vals[0]) for v in vals]
    tot = es[0] + es[1] + es[2] + es[3]
    member = jnp.zeros(logits.shape, F32)
    for k in range(TOP_K):
        member = member + jnp.where(col == idxs[k], 1.0, 0.0)
    before = carry_scr[...] + jnp.dot(tri_ref[...], member.astype(BF16), preferred_element_type=F32)
    carry_scr[...] = carry_scr[...] + jnp.sum(member, axis=0, keepdims=True)
    cnt_ref[...] = jnp.broadcast_to(carry_scr[...], cnt_ref.shape)

    idx_out = jnp.zeros((tm, LANES), I32)
    gate_out = jnp.zeros((tm, LANES), F32)
    rank_out = jnp.zeros((tm, LANES), I32)
    for k in range(TOP_K):
        rank_k = jnp.sum(jnp.where(col == idxs[k], before, 0.0), axis=-1, keepdims=True)
        idx_out = jnp.where(lane == k, idxs[k], idx_out)
        gate_out = jnp.where(lane == k, es[k] / tot, gate_out)
        rank_out = jnp.where(lane == k, rank_k.astype(I32), rank_out)
    idx_ref[...] = idx_out
    gate_ref[...] = gate_out
    rank_ref[...] = rank_out


def _router(x, w, b):
    n, d = x.shape
    tm = _row_tile(n, 640)
    row = lambda i: (i, 0)
    fixed = lambda i: (0, 0)
    tri = jnp.asarray(np.tril(np.ones((tm, tm), np.float32), -1), BF16)
    return pl.pallas_call(
        _router_body,
        grid=(n // tm,),
        in_specs=[pl.BlockSpec((tm, d), row), pl.BlockSpec((d, N_EXPERTS), fixed),
                  pl.BlockSpec((1, N_EXPERTS), fixed), pl.BlockSpec((tm, tm), fixed)],
        out_specs=[pl.BlockSpec((tm, LANES), row), pl.BlockSpec((tm, LANES), row),
                   pl.BlockSpec((tm, LANES), row), pl.BlockSpec((8, N_EXPERTS), fixed)],
        out_shape=[jax.ShapeDtypeStruct((n, LANES), I32), jax.ShapeDtypeStruct((n, LANES), F32),
                   jax.ShapeDtypeStruct((n, LANES), I32), jax.ShapeDtypeStruct((8, N_EXPERTS), F32)],
        scratch_shapes=[pltpu.VMEM((1, N_EXPERTS), F32)],
        compiler_params=_params(("arbitrary",)),
        name="router",
    )(x, w, b.reshape(1, N_EXPERTS), tri)


def _deinterleave_matrix():
    p = np.zeros((2 * LANES, 2 * LANES), np.float32)
    j = np.arange(LANES)
    p[2 * j, j] = 1.0
    p[2 * j + 1, LANES + j] = 1.0
    return p


def _dispatch_body(pos_ref, x_ref, xb_in, xb_hbm, buf, sem):
    del xb_in
    i = pl.program_id(0)
    nsteps = pl.num_programs(0)
    slot = lax.rem(i, 2)
    tm = DISPATCH_ROWS

    def wait(slot_):
        for _ in range(TOP_K):
            pltpu.make_async_copy(buf.at[slot_], xb_hbm.at[pl.ds(0, tm)], sem.at[slot_]).wait()

    @pl.when(i >= 2)
    def _():
        wait(slot)

    buf[slot] = x_ref[...]
    base = i * (tm * TOP_K)

    def body(g, c):
        for u in range(2):
            r = 2 * g + u
            for k in range(TOP_K):
                dst = pos_ref[base + r * TOP_K + k]
                pltpu.make_async_copy(buf.at[slot, pl.ds(r, 1)], xb_hbm.at[pl.ds(dst, 1)], sem.at[slot]).start()
        return c
    lax.fori_loop(0, tm // 2, body, 0)

    @pl.when(i == nsteps - 1)
    def _():
        wait(slot)

    @pl.when((i == nsteps - 1) & (i >= 1))
    def _():
        wait(1 - slot)


def _dispatch(x, pos_flat, n_rows):
    n, d = x.shape
    tm = DISPATCH_ROWS
    grid_spec = pltpu.PrefetchScalarGridSpec(
        num_scalar_prefetch=1,
        grid=(n // tm,),
        in_specs=[pl.BlockSpec((tm, d), lambda i, p: (i, 0)), pl.BlockSpec(memory_space=pl.ANY)],
        out_specs=pl.BlockSpec(memory_space=pl.ANY),
        scratch_shapes=[pltpu.VMEM((2, tm, d), F32), pltpu.SemaphoreType.DMA((2,))],
    )
    return pl.pallas_call(
        _dispatch_body,
        grid_spec=grid_spec,
        out_shape=jax.ShapeDtypeStruct((n_rows, d), F32),
        input_output_aliases={2: 0},
        compiler_params=_params(("arbitrary",)),
        name="dispatch",
    )(pos_flat, x, jnp.zeros((n_rows, d), F32))


def _expert_body(blk_e_ref, nblk_ref, x_ref, wu_ref, bg_ref, bl_ref, wd_ref, bd_ref, perm_ref, y_ref,
                 wg_scr, wl_scr, wd_scr):
    i = pl.program_id(0)
    nb = nblk_ref[0]
    changed = (i == 0) | (blk_e_ref[i] != blk_e_ref[jnp.maximum(i - 1, 0)])

    @pl.when((i < nb) & changed)
    def _():
        d = D_MODEL
        perm = perm_ref[...]
        for g in range(d // LANES):
            wblk = wu_ref[0, :, 2 * LANES * g:2 * LANES * (g + 1)].astype(BF16)
            t = jnp.dot(wblk, perm, preferred_element_type=F32)
            wg_scr[:, LANES * g:LANES * (g + 1)] = t[:, :LANES].astype(BF16)
            wl_scr[:, LANES * g:LANES * (g + 1)] = t[:, LANES:].astype(BF16)
        wd_scr[...] = wd_ref[0].astype(BF16)

    @pl.when(i < nb)
    def _():
        xb = x_ref[...].astype(BF16)
        hg = jnp.dot(xb, wg_scr[...], preferred_element_type=F32) + bg_ref[0]
        hl = jnp.dot(xb, wl_scr[...], preferred_element_type=F32) + bl_ref[0]
        glu = jnp.minimum(hg, SWIGLU_LIMIT)
        lin = jnp.clip(hl, -SWIGLU_LIMIT, SWIGLU_LIMIT)
        a = glu * jax.nn.sigmoid(SWIGLU_ALPHA * glu) * (lin + 1.0)
        y_ref[...] = jnp.dot(a.astype(BF16), wd_scr[...], preferred_element_type=F32) + bd_ref[0]

    @pl.when(i >= nb)
    def _():
        y_ref[...] = jnp.zeros(y_ref.shape, F32)


def _experts(xb, blk_e, n_used, w_up, bg, bl, w_down, bd):
    d = xb.shape[1]
    n_blk = blk_e.shape[0]
    w_map = lambda i, be, nb: (be[i], 0, 0)
    rows = lambda i, be, nb: (i, 0)
    perm = jnp.asarray(_deinterleave_matrix(), BF16)
    grid_spec = pltpu.PrefetchScalarGridSpec(
        num_scalar_prefetch=2,
        grid=(n_blk,),
        in_specs=[pl.BlockSpec((MOE_ROWS, d), rows),
                  pl.BlockSpec((1, d, 2 * d), w_map),
                  pl.BlockSpec((1, 1, d), w_map), pl.BlockSpec((1, 1, d), w_map),
                  pl.BlockSpec((1, d, d), w_map), pl.BlockSpec((1, 1, d), w_map),
                  pl.BlockSpec(perm.shape, lambda i, be, nb: (0, 0))],
        out_specs=pl.BlockSpec((MOE_ROWS, d), rows),
        scratch_shapes=[pltpu.VMEM((d, d), BF16), pltpu.VMEM((d, d), BF16), pltpu.VMEM((d, d), BF16)],
    )
    return pl.pallas_call(
        _expert_body,
        grid_spec=grid_spec,
        out_shape=jax.ShapeDtypeStruct((n_blk * MOE_ROWS, d), F32),
        compiler_params=_params(("arbitrary",)),
        name="experts",
    )(blk_e, n_used, xb, w_up, bg, bl, w_down, bd, perm)


def _combine_body(pos_ref, x_ref, gate_ref, y_hbm, g_ref, b_ref, o_ref, ybuf, sem):
    i = pl.program_id(0)
    nsteps = pl.num_programs(0)
    slot = lax.rem(i, 2)
    tm = COMBINE_ROWS

    def row_copy(src, slot_, k, r):
        return pltpu.make_async_copy(y_hbm.at[pl.ds(src, 1)], ybuf.at[slot_, k, pl.ds(r, 1)], sem.at[slot_])

    def issue(step, slot_):
        base = step * (tm * TOP_K)

        def body(g, c):
            for u in range(2):
                r = 2 * g + u
                for k in range(TOP_K):
                    row_copy(pos_ref[base + r * TOP_K + k], slot_, k, r).start()
            return c
        lax.fori_loop(0, tm // 2, body, 0)

    def wait(slot_):
        for k in range(TOP_K):
            pltpu.make_async_copy(y_hbm.at[pl.ds(0, tm)], ybuf.at[slot_, k], sem.at[slot_]).wait()

    @pl.when(i == 0)
    def _():
        issue(0, 0)

    @pl.when(i + 1 < nsteps)
    def _():
        issue(i + 1, 1 - slot)

    wait(slot)
    gates = gate_ref[...]
    m = gates[:, 0:1] * ybuf[slot, 0]
    for k in range(1, TOP_K):
        m = m + gates[:, k:k + 1] * ybuf[slot, k]
    o_ref[...] = _layer_norm(DN_ALPHA * x_ref[...] + m, g_ref[...], b_ref[...])


def _combine(x, gates, pos_flat, yb, g, b):
    n, d = x.shape
    tm = COMBINE_ROWS
    row = lambda i, p: (i, 0)
    fixed = lambda i, p: (0, 0)
    grid_spec = pltpu.PrefetchScalarGridSpec(
        num_scalar_prefetch=1,
        grid=(n // tm,),
        in_specs=[pl.BlockSpec((tm, d), row), pl.BlockSpec((tm, LANES), row),
                  pl.BlockSpec(memory_space=pl.ANY),
                  pl.BlockSpec((1, d), fixed), pl.BlockSpec((1, d), fixed)],
        out_specs=pl.BlockSpec((tm, d), row),
        scratch_shapes=[pltpu.VMEM((2, TOP_K, tm, d), F32), pltpu.SemaphoreType.DMA((2,))],
    )
    return pl.pallas_call(
        _combine_body,
        grid_spec=grid_spec,
        out_shape=jax.ShapeDtypeStruct((n, d), F32),
        compiler_params=_params(("arbitrary",)),
        name="combine",
    )(pos_flat, x, gates, yb, g.reshape(1, d), b.reshape(1, d))


def _moe(x, router_w, router_b, w_up, bg, bl, w_down, bd, ln_g, ln_b):
    n, d = x.shape
    idx_pad, gates, rank_pad, cnt = _router(x, router_w, router_b)
    nk = n * TOP_K
    counts = cnt[0].astype(I32)
    padded = (counts + MOE_ROWS - 1) // MOE_ROWS * MOE_ROWS
    ends = jnp.cumsum(padded)
    pad_start = ends - padded
    idx4 = idx_pad[:, :TOP_K]
    start4 = jnp.sum(jnp.where(idx4[..., None] == jnp.arange(N_EXPERTS), pad_start, 0), axis=-1)
    pos_flat = (start4 + rank_pad[:, :TOP_K]).reshape(nk).astype(I32)
    n_blk = -(-nk // MOE_ROWS) + N_EXPERTS
    blk_first = jnp.arange(n_blk, dtype=I32) * MOE_ROWS
    blk_e = jnp.minimum(jnp.sum(ends[None, :] <= blk_first[:, None], axis=1), N_EXPERTS - 1).astype(I32)
    n_used = (ends[-1:] // MOE_ROWS).astype(I32)
    xb = _dispatch(x, pos_flat, n_blk * MOE_ROWS)
    yb = _experts(xb, blk_e, n_used, w_up, bg, bl, w_down, bd)
    return _combine(x, gates, pos_flat, yb, ln_g, ln_b)


def _hgrn_cumsum_matrix(c):
    n_lv = int(math.log2(c))
    assert 1 << n_lv == c
    mats = []
    t = np.arange(c)
    for lv in range(n_lv):
        m = 1 << lv
        mat = np.zeros((c, c), np.float32)
        seg0 = (t // m) * m
        for r in range(c):
            if (r // m) % 2 == 1:
                mat[r, seg0[r]:r + 1] = 1.0
            else:
                mat[r, r + 1:seg0[r] + m] = 1.0
        mats.append(mat)
    mats.append(np.tril(np.ones((c, c), np.float32)))
    mats.append(np.triu(np.ones((c, c), np.float32), 1))
    return np.concatenate(mats, axis=0), n_lv


def _hgrn_body(c, n_lv, q_ref, f_ref, i_ref, g_ref, s0_ref, low_ref, nw_ref, l_ref,
               o_ref, sout_ref, st_scr, o_scr):
    step = pl.program_id(1)
    n_steps = pl.num_programs(1)

    @pl.when(step == 0)
    def _():
        for h in range(A_HEADS):
            st_scr[h] = s0_ref[0, h].T

    low = low_ref[...]
    f = low + (1.0 - low) * jax.nn.sigmoid(f_ref[0])
    lg = jnp.log(f)
    kk = 1.0 - f
    qz = q_ref[0]
    qq = qz * jax.nn.sigmoid(qz)
    vv = i_ref[0]

    hi = lg.astype(BF16)
    r1 = lg - hi.astype(F32)
    mid = r1.astype(BF16)
    lo = (r1 - mid.astype(F32)).astype(BF16)
    lmat = l_ref[...]
    z = (jnp.dot(lmat, hi, preferred_element_type=F32) + jnp.dot(lmat, mid, preferred_element_type=F32)
         + jnp.dot(lmat, lo, preferred_element_type=F32))

    row = lax.broadcasted_iota(I32, (c, 1), 0)
    rt = lax.broadcasted_iota(I32, (c, c), 0)
    rs = lax.broadcasted_iota(I32, (c, c), 1)
    qps, kps, same = [qq.astype(BF16)], [kk.astype(BF16)], [rt == rs]
    for lv in range(n_lv):
        e = jnp.exp(z[lv * c:(lv + 1) * c])
        upper = ((row >> lv) & 1) == 1
        qps.append(jnp.where(upper, qq * e, 0.0).astype(BF16))
        kps.append(jnp.where(upper, 0.0, kk * e).astype(BF16))
        same.append((rt >> (lv + 1)) == (rs >> (lv + 1)))
    b = z[n_lv * c:(n_lv + 1) * c]
    qb = (qq * jnp.exp(b)).astype(BF16)
    kc = (kk * jnp.exp(z[(n_lv + 1) * c:(n_lv + 2) * c])).astype(BF16)
    e_last = jnp.exp(b[c - 1:c, :])
    vb = vv.astype(BF16)

    for h in range(A_HEADS):
        sl = slice(h * A_DK, (h + 1) * A_DK)
        a = jnp.zeros((c, c), F32)
        for qp, kp, sm in zip(qps, kps, same):
            a = a + jnp.where(sm, _dot_nt(qp[:, sl], kp[:, sl]), 0.0)
        st = st_scr[h]
        o_h = _dot_nt(qb[:, sl], st.astype(BF16)) + jnp.dot(a.astype(BF16), vb[:, sl],
                                                             preferred_element_type=F32)
        o_scr[:, sl] = o_h
        upd = lax.dot_general(vb[:, sl], kc[:, sl], (((0,), (0,)), ((), ())), preferred_element_type=F32)
        st_scr[h] = st * e_last[:, sl] + upd

    o = o_scr[...]
    o = o * lax.rsqrt(jnp.mean(o * o, axis=-1, keepdims=True) + RMS_EPS) * nw_ref[...]
    gz = g_ref[0]
    o_ref[0] = o * (gz * jax.nn.sigmoid(gz))

    @pl.when(step == n_steps - 1)
    def _():
        for h in range(A_HEADS):
            sout_ref[0, h] = st_scr[h].T


def _hgrn(proj, s0, lower, norm_w, c):
    bsz, t, _ = proj.shape
    d = D_MODEL
    lmat, n_lv = _hgrn_cumsum_matrix(c)
    lmat = jnp.asarray(lmat, BF16)
    blk = lambda col: pl.BlockSpec((1, c, d), lambda b, s, col=col: (b, s, col))
    st_spec = pl.BlockSpec((1, A_HEADS, A_DK, A_DK), lambda b, s: (b, 0, 0, 0))
    vec = pl.BlockSpec((1, d), lambda b, s: (0, 0))
    return pl.pallas_call(
        functools.partial(_hgrn_body, c, n_lv),
        grid=(bsz, t // c),
        in_specs=[blk(0), blk(1), blk(2), blk(3), st_spec, vec, vec,
                  pl.BlockSpec(lmat.shape, lambda b, s: (0, 0))],
        out_specs=[pl.BlockSpec((1, c, d), lambda b, s: (b, s, 0)), st_spec],
        out_shape=[jax.ShapeDtypeStruct((bsz, t, d), F32), jax.ShapeDtypeStruct(s0.shape, F32)],
        scratch_shapes=[pltpu.VMEM((A_HEADS, A_DK, A_DK), F32), pltpu.VMEM((c, d), F32)],
        compiler_params=_params(("arbitrary", "arbitrary")),
        name=f"hgrn_c{c}",
    )(proj, proj, proj, proj, s0, lower.reshape(1, d), norm_w.reshape(1, d), lmat)


def _bias_by_distance(rel_bias, n):
    dist = jnp.arange(n, dtype=I32)
    max_exact = N_BUCKETS // 2
    scaled = jnp.log(jnp.maximum(dist, max_exact).astype(F32) / max_exact) / math.log(MAX_DISTANCE / max_exact)
    large = jnp.minimum(max_exact + (scaled * (N_BUCKETS - max_exact)).astype(I32), N_BUCKETS - 1)
    bucket = jnp.where(dist < max_exact, dist, large)
    return rel_bias.astype(F32)[bucket].T


def _top3_rows(g, rowb, n_rows):
    sel = jnp.zeros(g.shape, F32)
    for _ in range(MOBA_TOPK):
        m = jnp.max(g, axis=0, keepdims=True)
        ix = jnp.min(jnp.where(g == m, rowb, n_rows), axis=0, keepdims=True)
        pick = rowb == ix
        sel = jnp.where(pick & (m > -jnp.inf), 1.0, sel)
        g = jnp.where(pick, -jnp.inf, g)
    return sel


def _moba_p_body(n_blk, farb_ref, q_ref, k_ref, vt_ref, bd_ref, bp_ref, o_ref,
                 km_scr, row_scr, s_scr, sd_scr, mx_scr, l_scr, acc_scr):
    hp = pl.program_id(1)
    i = pl.program_id(2)
    blk = MOBA_BLOCK

    @pl.when(i == 0)
    def _():
        def body(n, c):
            kb = k_ref[0, pl.ds(pl.multiple_of(n * blk, blk), blk), :]
            km_scr[pl.ds(n, 1), :] = jnp.mean(kb, axis=0, keepdims=True)
            return c
        lax.fori_loop(0, n_blk, body, 0)

    q = q_ref[0] * (B_HEAD_DIM ** -0.5)
    lane = lax.broadcasted_iota(I32, q.shape, 1)
    qh = [jnp.where(lane < B_HEAD_DIM, q, 0.0), jnp.where(lane >= B_HEAD_DIM, q, 0.0)]
    qhb = [(x * LOG2E).astype(BF16) for x in qh]
    km = km_scr[...]
    rowb = lax.broadcasted_iota(I32, (n_blk, blk), 0)
    for a in range(2):
        g = _dot_nt(km, qh[a], precision=HIGHEST)
        g = jnp.where(rowb < i, g, -jnp.inf)
        row_scr[a] = jnp.where(_top3_rows(g, rowb, n_blk) > 0.0, 0.0, NEG_INF)

    def k_block(j):
        return k_ref[0, pl.ds(pl.multiple_of(j * blk, blk), blk), :].astype(BF16)

    def col_max8(s):
        return jnp.max(s.reshape(blk // 8, 8, blk), axis=0)

    kd = k_block(i)
    for a in range(2):
        s = _dot_nt(kd, qhb[a]) + bd_ref[a]
        sd_scr[a] = s
        mx_scr[a] = col_max8(s)

    n_far = jnp.maximum(i - 1, 0)

    def far_logits(t, c):
        for u in range(2):
            j = 2 * t + u
            kb = k_block(j)
            live = j < n_far
            for a in range(2):
                row = jnp.where(live, row_scr[a, pl.ds(j, 1), :] + farb_ref[2 * hp + a], NEG_INF)
                s = _dot_nt(kb, qhb[a]) + row
                s_scr[a, j] = s
                mx_scr[a] = jnp.maximum(mx_scr[a], col_max8(s))
        return c
    lax.fori_loop(0, (n_far + 1) // 2, far_logits, 0)

    @pl.when(i >= 1)
    def _():
        kb = k_block(i - 1)
        for a in range(2):
            s = _dot_nt(kb, qhb[a]) + bp_ref[a] + row_scr[a, pl.ds(i - 1, 1), :]
            s_scr[a, i - 1] = s
            mx_scr[a] = jnp.maximum(mx_scr[a], col_max8(s))

    m = [jnp.max(mx_scr[a], axis=0, keepdims=True) for a in range(2)]
    vtd = vt_ref[0, i]
    for a in range(2):
        p = jnp.exp2(sd_scr[a] - m[a])
        l_scr[a] = jnp.sum(p, axis=0, keepdims=True)
        acc_scr[a] = jnp.dot(vtd, p.astype(BF16), preferred_element_type=F32)

    def past_values(t, c):
        lsum = [jnp.zeros((1, blk), F32), jnp.zeros((1, blk), F32)]
        pv = [jnp.zeros((LANES, blk), F32), jnp.zeros((LANES, blk), F32)]
        for u in range(2):
            j = 2 * t + u
            live = j < i
            jc = jnp.where(live, j, 0)
            vtb = vt_ref[0, jc]
            for a in range(2):
                p = jnp.where(live, jnp.exp2(s_scr[a, jc] - m[a]), 0.0)
                lsum[a] = lsum[a] + jnp.sum(p, axis=0, keepdims=True)
                pv[a] = pv[a] + jnp.dot(vtb, p.astype(BF16), preferred_element_type=F32)
        for a in range(2):
            l_scr[a] = l_scr[a] + lsum[a]
            acc_scr[a] = acc_scr[a] + pv[a]
        return c
    lax.fori_loop(0, (i + 1) // 2, past_values, 0)

    out0 = acc_scr[0] / l_scr[0]
    out1 = acc_scr[1] / l_scr[1]
    sub = lax.broadcasted_iota(I32, (LANES, blk), 0)
    o_ref[0] = jnp.where(sub < B_HEAD_DIM, out0, out1).T


def _moba_prompt(q, kv, vt, bias_tbl):
    bsz, t, d = q.shape
    blk = MOBA_BLOCK
    n_blk = t // blk
    n_hp = B_HEADS // 2
    dd = jnp.arange(blk)[None, :] - jnp.arange(blk)[:, None]
    bias2 = bias_tbl * LOG2E
    bias_diag = jnp.where(dd >= 0, bias2[:, jnp.maximum(dd, 0)], NEG_INF)
    bias_prev = bias2[:, dd + blk]
    far_bias = bias2[:, 2 * blk - 1]
    return pl.pallas_call(
        functools.partial(_moba_p_body, n_blk),
        grid=(bsz, n_hp, n_blk),
        in_specs=[pl.BlockSpec(memory_space=pltpu.SMEM),
                  pl.BlockSpec((1, blk, LANES), lambda b, h, i: (b, i, h)),
                  pl.BlockSpec((1, t, LANES), lambda b, h, i: (b, 0, h)),
                  pl.BlockSpec((1, n_blk, LANES, blk), lambda b, h, i: (b, 0, h, 0)),
                  pl.BlockSpec((2, blk, blk), lambda b, h, i: (h, 0, 0)),
                  pl.BlockSpec((2, blk, blk), lambda b, h, i: (h, 0, 0))],
        out_specs=pl.BlockSpec((1, blk, LANES), lambda b, h, i: (b, i, h)),
        out_shape=jax.ShapeDtypeStruct((bsz, t, d), F32),
        scratch_shapes=[pltpu.VMEM((n_blk, LANES), F32), pltpu.VMEM((2, n_blk, blk), F32),
                        pltpu.VMEM((2, n_blk, blk, blk), F32), pltpu.VMEM((2, blk, blk), F32),
                        pltpu.VMEM((2, 8, blk), F32), pltpu.VMEM((2, 1, blk), F32),
                        pltpu.VMEM((2, LANES, blk), F32)],
        compiler_params=_params(("arbitrary", "arbitrary", "arbitrary")),
        name="moba_prompt",
    )(far_bias, q, kv, vt, bias_diag, bias_prev)


def _moba_s_body(n_pages, t_new, pt_ref, qbd_ref, *refs):
    del pt_ref
    g = SAMPLE_PAGES_PER_STEP
    kc_refs, vc_refs = refs[:g], refs[g:2 * g]
    (kn_ref, vn_ref, blast_ref, bnew_ref, farb_ref, o_ref,
     s_scr, km_scr, sel_scr, pn_scr, vn_scr, acc_scr, qb_scr) = refs[2 * g:]
    p = pl.program_id(1)
    page = kc_refs[0].shape[1]
    pages_per_blk = MOBA_BLOCK // page
    n_blk = n_pages // pages_per_blk
    n_steps = n_pages // g

    @pl.when(p == 0)
    def _():
        qb_scr[...] = qbd_ref[0].astype(BF16)

    @pl.when(p < n_steps)
    def _():
        ks = [r[0] for r in kc_refs]
        s = jnp.dot(jnp.concatenate(ks, axis=0).astype(BF16), qb_scr[...], preferred_element_type=F32)
        s_scr[pl.ds(p * g, g)] = s.reshape(g, page, LANES)
        for bi in range(g // pages_per_blk):
            tot = jnp.sum(ks[bi * pages_per_blk], axis=0, keepdims=True)
            for u in range(1, pages_per_blk):
                tot = tot + jnp.sum(ks[bi * pages_per_blk + u], axis=0, keepdims=True)
            km_scr[pl.ds(p * (g // pages_per_blk) + bi, 1), :] = tot

    @pl.when(p == n_steps - 1)
    def _():
        km = km_scr[...] * (1.0 / MOBA_BLOCK)
        g = jnp.dot(km, qbd_ref[0], precision=HIGHEST, preferred_element_type=F32)
        rowb = lax.broadcasted_iota(I32, g.shape, 0)
        sel_scr[...] = _top3_rows(g, rowb, n_blk)

        sn = jnp.dot(kn_ref[0].astype(BF16), qb_scr[...], preferred_element_type=F32) + bnew_ref[...]
        kj = lax.broadcasted_iota(I32, sn.shape, 0)
        qi = lax.rem(lax.broadcasted_iota(I32, sn.shape, 1), t_new)
        sn = jnp.where(kj <= qi, sn, NEG_INF)
        farb = farb_ref[...]

        def pass_max(pg, m):
            bias = jnp.where(pg == n_pages - 1, blast_ref[...], farb)
            keep = sel_scr[pl.ds(pg // pages_per_blk, 1), :] > 0.0
            s = jnp.where(keep, s_scr[pg] + bias, NEG_INF)
            s_scr[pg] = s
            return jnp.maximum(m, jnp.max(s, axis=0, keepdims=True))
        m = lax.fori_loop(0, n_pages, pass_max, jnp.max(sn, axis=0, keepdims=True))

        def pass_exp(pg, l):
            e = jnp.exp(s_scr[pg] - m)
            s_scr[pg] = e
            return l + jnp.sum(e, axis=0, keepdims=True)
        en = jnp.exp(sn - m)
        l = lax.fori_loop(0, n_pages, pass_exp, jnp.sum(en, axis=0, keepdims=True))
        inv = 1.0 / l

        def pass_norm(pg, c):
            s_scr[pg] = s_scr[pg] * inv
            return c
        lax.fori_loop(0, n_pages, pass_norm, 0)
        pn_scr[...] = jnp.zeros(pn_scr.shape, F32)
        pn_scr[0:t_new, :] = en * inv
        vn_scr[...] = jnp.zeros(vn_scr.shape, F32)
        vn_scr[0:t_new, :] = vn_ref[0]
        acc_scr[...] = jnp.zeros(acc_scr.shape, F32)

    @pl.when(p >= n_steps)
    def _():
        pg0 = (p - n_steps) * g
        pt = jnp.concatenate([s_scr[pg0 + u].T for u in range(g)], axis=1).astype(BF16)
        vcat = jnp.concatenate([r[0] for r in vc_refs], axis=0).astype(BF16)
        acc_scr[...] = acc_scr[...] + jnp.dot(pt, vcat, preferred_element_type=F32)

    @pl.when(p == 2 * n_steps - 1)
    def _():
        acc = acc_scr[...] + jnp.dot(pn_scr[...].T.astype(BF16), vn_scr[...].astype(BF16),
                                     preferred_element_type=F32)
        rh = lax.broadcasted_iota(I32, acc.shape, 0) // t_new
        ch = lax.broadcasted_iota(I32, acc.shape, 1) // B_HEAD_DIM
        acc = jnp.where(rh == ch, acc, 0.0)
        o_ref[0] = jnp.sum(acc.reshape(B_HEADS, t_new, D_MODEL), axis=0)


def _moba_sample(q, kv, cache_k, cache_v, page_table, bias_tbl):
    s, t_new, d = q.shape
    n_pages = page_table.shape[1]
    page = cache_k.shape[1]
    assert B_HEADS * t_new == LANES and page == LANES and MOBA_BLOCK % page == 0
    n_blk = n_pages // (MOBA_BLOCK // page)
    scale = B_HEAD_DIM ** -0.5
    qh = (q * scale).reshape(s, t_new, B_HEADS, B_HEAD_DIM).transpose(0, 2, 3, 1)
    eye = jnp.eye(B_HEADS, dtype=F32)
    qbd = (qh[:, :, :, None, :] * eye[None, :, None, :, None]).reshape(s, d, LANES)
    hq_h = jnp.arange(LANES) // t_new
    hq_i = jnp.arange(LANES) % t_new
    keys = jnp.arange(page)
    bias_last = bias_tbl[hq_h[None, :], (page + hq_i[None, :] - keys[:, None])]
    kj = jnp.arange(t_new)
    bias_new = bias_tbl[hq_h[None, :], jnp.maximum(hq_i[None, :] - kj[:, None], 0)]
    far_bias = bias_tbl[hq_h, 2 * MOBA_BLOCK - 1].reshape(1, LANES)
    k_new = kv[..., :d]
    v_new = kv[..., d:]
    g = SAMPLE_PAGES_PER_STEP
    assert n_pages % g == 0 and g % (MOBA_BLOCK // page) == 0
    n_steps = n_pages // g
    kmaps = [lambda b, p, pt, u=u: (pt[b * n_pages + g * jnp.minimum(p, n_steps - 1) + u], 0, 0)
             for u in range(g)]
    vmaps = [lambda b, p, pt, u=u: (pt[b * n_pages + g * jnp.maximum(p - n_steps, 0) + u], 0, 0)
             for u in range(g)]
    seq = lambda b, p, pt: (b, 0, 0)
    fixed = lambda b, p, pt: (0, 0)
    grid_spec = pltpu.PrefetchScalarGridSpec(
        num_scalar_prefetch=1,
        grid=(s, 2 * n_steps),
        in_specs=[pl.BlockSpec((1, d, LANES), seq)]
                 + [pl.BlockSpec((1, page, d), m) for m in kmaps]
                 + [pl.BlockSpec((1, page, d), m) for m in vmaps]
                 + [pl.BlockSpec((1, t_new, d), seq), pl.BlockSpec((1, t_new, d), seq),
                    pl.BlockSpec((page, LANES), fixed), pl.BlockSpec((t_new, LANES), fixed),
                    pl.BlockSpec((1, LANES), fixed)],
        out_specs=pl.BlockSpec((1, t_new, d), seq),
        scratch_shapes=[pltpu.VMEM((n_pages, page, LANES), F32), pltpu.VMEM((n_blk, d), F32),
                        pltpu.VMEM((n_blk, LANES), F32), pltpu.VMEM((page, LANES), F32),
                        pltpu.VMEM((page, d), F32), pltpu.VMEM((LANES, d), F32),
                        pltpu.VMEM((d, LANES), BF16)],
    )
    return pl.pallas_call(
        functools.partial(_moba_s_body, n_pages, t_new),
        grid_spec=grid_spec,
        out_shape=jax.ShapeDtypeStruct((s, t_new, d), F32),
        compiler_params=_params(("arbitrary", "arbitrary")),
        name="moba_sample",
    )(page_table.reshape(-1).astype(I32), qbd, *([cache_k] * g), *([cache_v] * g),
      k_new, v_new, bias_last, bias_new, far_bias)


def kernel(x_prompt, x_sample, cache_k, cache_v, state_hgrn, page_table, a_w_in, a_lb, a_norm, a_w_out,
           kv_w, b_w_q, b_w_o, rel_bias, router_w, router_b, w_up, b_up, w_down, b_down, ln_g, ln_b):
    bp, tp, d = x_prompt.shape
    bs, ts, _ = x_sample.shape
    n_p = bp * tp
    n_s = bs * ts
    x = jnp.concatenate([x_prompt.reshape(n_p, d), x_sample.reshape(n_s, d)], axis=0)

    p_lb = jax.nn.softmax(a_lb.astype(F32), axis=0)
    lower = jnp.cumsum(p_lb, axis=0) - p_lb[0]
    bias_tbl = _bias_by_distance(rel_bias, 2 * MOBA_BLOCK)
    n_phys, page = cache_k.shape[:2]
    ck = cache_k.reshape(n_phys, page, d)
    cv = cache_v.reshape(n_phys, page, d)
    zero_state = jnp.zeros((bp,) + state_hgrn.shape[2:], F32)
    hgrn_chunk = 64 if tp % 64 == 0 else tp

    states_p, states_s = [], []
    kv = None
    for l in range(DEPTH):
        if l < N_A_LAYERS:
            proj = _proj(x, a_w_in[l].astype(BF16))
            o_p, s_p = _hgrn(proj[:n_p].reshape(bp, tp, 4 * d), zero_state, lower[l], a_norm[l], hgrn_chunk)
            o_s, s_s = _hgrn(proj[n_p:].reshape(bs, ts, 4 * d), state_hgrn[l], lower[l], a_norm[l], ts)
            states_p.append(s_p)
            states_s.append(s_s)
            mix = jnp.concatenate([o_p.reshape(n_p, d), o_s.reshape(n_s, d)], axis=0)
            x = _mm_res_ln(x, mix, a_w_out[l].astype(BF16), ln_g[l, 0], ln_b[l, 0])
        else:
            j = l - N_A_LAYERS
            if kv is None:
                kv = _proj(x, kv_w.astype(BF16))
                kv_p = kv[:n_p].reshape(bp, tp, 2 * d)
                kv_s = kv[n_p:].reshape(bs, ts, 2 * d)
                n_kb = tp // MOBA_BLOCK
                vt_p = kv_p[..., d:].astype(BF16).reshape(bp, n_kb, MOBA_BLOCK, d).transpose(0, 1, 3, 2)
            q = _proj(x, b_w_q[j].astype(BF16))
            o_p = _moba_prompt(q[:n_p].reshape(bp, tp, d), kv_p, vt_p, bias_tbl)
            o_s = _moba_sample(q[n_p:].reshape(bs, ts, d), kv_s, ck, cv, page_table, bias_tbl)
            mix = jnp.concatenate([o_p.reshape(n_p, d), o_s.reshape(n_s, d)], axis=0)
            x = _mm_res_ln(x, mix, b_w_o[j].astype(BF16), ln_g[l, 0], ln_b[l, 0])
        bg = b_up[l, :, 0::2].reshape(N_EXPERTS, 1, d)
        bl = b_up[l, :, 1::2].reshape(N_EXPERTS, 1, d)
        x = _moe(x, router_w[l], router_b[l], w_up[l], bg, bl, w_down[l],
                 b_down[l].reshape(N_EXPERTS, 1, d), ln_g[l, 1], ln_b[l, 1])

    y_p = x[:n_p].reshape(bp, tp, d)
    y_s = x[n_p:].reshape(bs, ts, d)
    heads = (B_HEADS, B_HEAD_DIM)
    return (y_p, y_s,
            kv_p[..., :d].reshape(bp, tp, *heads), kv_p[..., d:].reshape(bp, tp, *heads),
            jnp.stack(states_p),
            kv_s[..., :d].reshape(bs, ts, *heads), kv_s[..., d:].reshape(bs, ts, *heads),
            jnp.stack(states_s))
```

```python
import functools
import math

import jax
import jax.numpy as jnp
import numpy as np
from jax import lax
from jax.experimental import pallas as pl
from jax.experimental.pallas import tpu as pltpu

F32 = jnp.float32
BF16 = jnp.bfloat16
I32 = jnp.int32
HIGHEST = lax.Precision.HIGHEST

D_MODEL = 1024
DEPTH = 4
N_A_LAYERS = 2
A_HEADS = 8
A_DK = 128
B_HEADS = 16
B_HEAD_DIM = 64
MOBA_BLOCK = 256
MOBA_TOPK = 3
N_BUCKETS = 32
MAX_DISTANCE = 128
N_EXPERTS = 32
TOP_K = 4
SWIGLU_LIMIT = 7.0
SWIGLU_ALPHA = 1.702
DN_ALPHA = (2 * DEPTH) ** 0.25
LN_EPS = 1e-5
RMS_EPS = 1e-6
NEG_INF = -1e30
LOG2E = math.log2(math.e)

LANES = 128
VMEM_LIMIT = 56 * 1024 * 1024
MOE_ROWS = 256
COMBINE_ROWS = 128
DISPATCH_ROWS = 128
SAMPLE_PAGES_PER_STEP = 8


def _params(sem, vmem=VMEM_LIMIT):
    return pltpu.CompilerParams(dimension_semantics=sem, vmem_limit_bytes=vmem)


def _row_tile(n, cap):
    best = None
    for t in range(8, min(n, cap) + 1, 8):
        if n % t == 0:
            best = t
    assert best is not None, n
    return best


def _dot_nt(a, b, precision=None):
    return lax.dot_general(a, b, (((1,), (1,)), ((), ())), precision=precision,
                           preferred_element_type=F32)


def _layer_norm(y, g, b):
    mu = jnp.mean(y, axis=-1, keepdims=True)
    yc = y - mu
    var = jnp.mean(yc * yc, axis=-1, keepdims=True)
    return yc * lax.rsqrt(var + LN_EPS) * g + b


def _proj_body(x_ref, w_ref, o_ref):
    o_ref[...] = jnp.dot(x_ref[...].astype(BF16), w_ref[...], preferred_element_type=F32)


def _proj(x, w_bf16):
    n, k = x.shape
    m = w_bf16.shape[1]
    tn = 1024
    tm = _row_tile(n, 640)
    return pl.pallas_call(
        _proj_body,
        grid=(m // tn, n // tm),
        in_specs=[pl.BlockSpec((tm, k), lambda j, i: (i, 0)),
                  pl.BlockSpec((k, tn), lambda j, i: (0, j))],
        out_specs=pl.BlockSpec((tm, tn), lambda j, i: (i, j)),
        out_shape=jax.ShapeDtypeStruct((n, m), F32),
        compiler_params=_params(("arbitrary", "arbitrary")),
        name="proj",
    )(x, w_bf16)


def _kv_proj_body(x_ref, w_ref, kv_ref, vt_ref, km_ref):
    y = jnp.dot(x_ref[...].astype(BF16), w_ref[...], preferred_element_type=F32)
    kv_ref[...] = y
    km_ref[0] = jnp.mean(y[:, :D_MODEL], axis=0, keepdims=True)
    vt_ref[0] = y[:, D_MODEL:].T.astype(BF16)


def _kv_proj(x, w_bf16):
    n, k = x.shape
    d = D_MODEL
    tm = MOBA_BLOCK
    assert n % tm == 0
    return pl.pallas_call(
        _kv_proj_body,
        grid=(n // tm,),
        in_specs=[pl.BlockSpec((tm, k), lambda i: (i, 0)), pl.BlockSpec((k, 2 * d), lambda i: (0, 0))],
        out_specs=[pl.BlockSpec((tm, 2 * d), lambda i: (i, 0)),
                   pl.BlockSpec((1, d, tm), lambda i: (i, 0, 0)),
                   pl.BlockSpec((1, 1, d), lambda i: (i, 0, 0))],
        out_shape=[jax.ShapeDtypeStruct((n, 2 * d), F32),
                   jax.ShapeDtypeStruct((n // tm, d, tm), BF16),
                   jax.ShapeDtypeStruct((n // tm, 1, d), F32)],
        compiler_params=_params(("arbitrary",)),
        name="kv_proj",
    )(x, w_bf16)


def _mm_res_ln_body(x_ref, a_ref, w_ref, g_ref, b_ref, o_ref):
    h = jnp.dot(a_ref[...].astype(BF16), w_ref[...], preferred_element_type=F32)
    o_ref[...] = _layer_norm(DN_ALPHA * x_ref[...] + h, g_ref[...], b_ref[...])


def _mm_res_ln(x, a, w_bf16, g, b):
    n, d = x.shape
    tm = _row_tile(n, 640)
    row = lambda i: (i, 0)
    fixed = lambda i: (0, 0)
    return pl.pallas_call(
        _mm_res_ln_body,
        grid=(n // tm,),
        in_specs=[pl.BlockSpec((tm, d), row), pl.BlockSpec((tm, d), row),
                  pl.BlockSpec((d, d), fixed), pl.BlockSpec((1, d), fixed), pl.BlockSpec((1, d), fixed)],
        out_specs=pl.BlockSpec((tm, d), row),
        out_shape=jax.ShapeDtypeStruct((n, d), F32),
        compiler_params=_params(("arbitrary",)),
        name="mm_res_ln",
    )(x, a, w_bf16, g.reshape(1, d), b.reshape(1, d))


def _router_body(x_ref, w_ref, b_ref, tri_ref, idx_ref, gate_ref, rank_ref, cnt_ref, carry_scr):
    @pl.when(pl.program_id(0) == 0)
    def _():
        carry_scr[...] = jnp.zeros(carry_scr.shape, F32)

    logits = jnp.dot(x_ref[...], w_ref[...], precision=HIGHEST, preferred_element_type=F32) + b_ref[...]
    tm = logits.shape[0]
    col = lax.broadcasted_iota(I32, logits.shape, 1)
    lane = lax.broadcasted_iota(I32, (tm, LANES), 1)
    vals, idxs = [], []
    cur = logits
    for _ in range(TOP_K):
        m = jnp.max(cur, axis=-1, keepdims=True)
        ix = jnp.min(jnp.where(cur == m, col, N_EXPERTS), axis=-1, keepdims=True)
        vals.append(m)
        idxs.append(ix)
        cur = jnp.where(col == ix, -jnp.inf, cur)
    es = [jnp.exp(v - vals[0]) for v in vals]
    tot = es[0] + es[1] + es[2] + es[3]
    member = jnp.zeros(logits.shape, F32)
    for k in range(TOP_K):
        member = member + jnp.where(col == idxs[k], 1.0, 0.0)
    before = carry_scr[...] + jnp.dot(tri_ref[...], member.astype(BF16), preferred_element_type=F32)
    carry_scr[...] = carry_scr[...] + jnp.sum(member, axis=0, keepdims=True)
    cnt_ref[...] = jnp.broadcast_to(carry_scr[...], cnt_ref.shape)

    idx_out = jnp.zeros((tm, LANES), I32)
    gate_out = jnp.zeros((tm, LANES), F32)
    rank_out = jnp.zeros((tm, LANES), I32)
    for k in range(TOP_K):
        rank_k = jnp.sum(jnp.where(col == idxs[k], before, 0.0), axis=-1, keepdims=True)
        idx_out = jnp.where(lane == k, idxs[k], idx_out)
        gate_out = jnp.where(lane == k, es[k] / tot, gate_out)
        rank_out = jnp.where(lane == k, rank_k.astype(I32), rank_out)
    idx_ref[...] = idx_out
    gate_ref[...] = gate_out
    rank_ref[...] = rank_out


def _router(x, w, b):
    n, d = x.shape
    tm = _row_tile(n, 640)
    row = lambda i: (i, 0)
    fixed = lambda i: (0, 0)
    tri = jnp.asarray(np.tril(np.ones((tm, tm), np.float32), -1), BF16)
    return pl.pallas_call(
        _router_body,
        grid=(n // tm,),
        in_specs=[pl.BlockSpec((tm, d), row), pl.BlockSpec((d, N_EXPERTS), fixed),
                  pl.BlockSpec((1, N_EXPERTS), fixed), pl.BlockSpec((tm, tm), fixed)],
        out_specs=[pl.BlockSpec((tm, LANES), row), pl.BlockSpec((tm, LANES), row),
                   pl.BlockSpec((tm, LANES), row), pl.BlockSpec((8, N_EXPERTS), fixed)],
        out_shape=[jax.ShapeDtypeStruct((n, LANES), I32), jax.ShapeDtypeStruct((n, LANES), F32),
                   jax.ShapeDtypeStruct((n, LANES), I32), jax.ShapeDtypeStruct((8, N_EXPERTS), F32)],
        scratch_shapes=[pltpu.VMEM((1, N_EXPERTS), F32)],
        compiler_params=_params(("arbitrary",)),
        name="router",
    )(x, w, b.reshape(1, N_EXPERTS), tri)


def _deinterleave_matrix():
    p = np.zeros((2 * LANES, 2 * LANES), np.float32)
    j = np.arange(LANES)
    p[2 * j, j] = 1.0
    p[2 * j + 1, LANES + j] = 1.0
    return p


def _dispatch_body(pos_ref, x_ref, xb_in, xb_hbm, buf, sem):
    del xb_in
    i = pl.program_id(0)
    nsteps = pl.num_programs(0)
    slot = lax.rem(i, 2)
    tm = DISPATCH_ROWS

    def wait(slot_):
        for _ in range(TOP_K):
            pltpu.make_async_copy(buf.at[slot_], xb_hbm.at[pl.ds(0, tm)], sem.at[slot_]).wait()

    @pl.when(i >= 2)
    def _():
        wait(slot)

    buf[slot] = x_ref[...]
    base = i * (tm * TOP_K)

    def body(g, c):
        for u in range(2):
            r = 2 * g + u
            for k in range(TOP_K):
                dst = pos_ref[base + r * TOP_K + k]
                pltpu.make_async_copy(buf.at[slot, pl.ds(r, 1)], xb_hbm.at[pl.ds(dst, 1)], sem.at[slot]).start()
        return c
    lax.fori_loop(0, tm // 2, body, 0)

    @pl.when(i == nsteps - 1)
    def _():
        wait(slot)

    @pl.when((i == nsteps - 1) & (i >= 1))
    def _():
        wait(1 - slot)


def _dispatch(x, pos_flat, n_rows):
    n, d = x.shape
    tm = DISPATCH_ROWS
    grid_spec = pltpu.PrefetchScalarGridSpec(
        num_scalar_prefetch=1,
        grid=(n // tm,),
        in_specs=[pl.BlockSpec((tm, d), lambda i, p: (i, 0)), pl.BlockSpec(memory_space=pl.ANY)],
        out_specs=pl.BlockSpec(memory_space=pl.ANY),
        scratch_shapes=[pltpu.VMEM((2, tm, d), F32), pltpu.SemaphoreType.DMA((2,))],
    )
    return pl.pallas_call(
        _dispatch_body,
        grid_spec=grid_spec,
        out_shape=jax.ShapeDtypeStruct((n_rows, d), F32),
        input_output_aliases={2: 0},
        compiler_params=_params(("arbitrary",)),
        name="dispatch",
    )(pos_flat, x, jnp.zeros((n_rows, d), F32))


def _expert_body(blk_e_ref, nblk_ref, x_ref, wu_ref, bg_ref, bl_ref, wd_ref, bd_ref, perm_ref, y_ref,
                 wg_scr, wl_scr, wd_scr):
    i = pl.program_id(0)
    nb = nblk_ref[0]
    changed = (i == 0) | (blk_e_ref[i] != blk_e_ref[jnp.maximum(i - 1, 0)])

    @pl.when((i < nb) & changed)
    def _():
        d = D_MODEL
        perm = perm_ref[...]
        for g in range(d // LANES):
            wblk = wu_ref[0, :, 2 * LANES * g:2 * LANES * (g + 1)].astype(BF16)
            t = jnp.dot(wblk, perm, preferred_element_type=F32)
            wg_scr[:, LANES * g:LANES * (g + 1)] = t[:, :LANES].astype(BF16)
            wl_scr[:, LANES * g:LANES * (g + 1)] = t[:, LANES:].astype(BF16)
        wd_scr[...] = wd_ref[0].astype(BF16)

    @pl.when(i < nb)
    def _():
        xb = x_ref[...].astype(BF16)
        hg = jnp.dot(xb, wg_scr[...], preferred_element_type=F32) + bg_ref[0]
        hl = jnp.dot(xb, wl_scr[...], preferred_element_type=F32) + bl_ref[0]
        glu = jnp.minimum(hg, SWIGLU_LIMIT)
        lin = jnp.clip(hl, -SWIGLU_LIMIT, SWIGLU_LIMIT)
        a = glu * jax.nn.sigmoid(SWIGLU_ALPHA * glu) * (lin + 1.0)
        y_ref[...] = jnp.dot(a.astype(BF16), wd_scr[...], preferred_element_type=F32) + bd_ref[0]

    @pl.when(i >= nb)
    def _():
        y_ref[...] = jnp.zeros(y_ref.shape, F32)


def _experts(xb, blk_e, n_used, w_up, bg, bl, w_down, bd):
    d = xb.shape[1]
    n_blk = blk_e.shape[0]
    w_map = lambda i, be, nb: (be[i], 0, 0)
    rows = lambda i, be, nb: (i, 0)
    perm = jnp.asarray(_deinterleave_matrix(), BF16)
    grid_spec = pltpu.PrefetchScalarGridSpec(
        num_scalar_prefetch=2,
        grid=(n_blk,),
        in_specs=[pl.BlockSpec((MOE_ROWS, d), rows),
                  pl.BlockSpec((1, d, 2 * d), w_map),
                  pl.BlockSpec((1, 1, d), w_map), pl.BlockSpec((1, 1, d), w_map),
                  pl.BlockSpec((1, d, d), w_map), pl.BlockSpec((1, 1, d), w_map),
                  pl.BlockSpec(perm.shape, lambda i, be, nb: (0, 0))],
        out_specs=pl.BlockSpec((MOE_ROWS, d), rows),
        scratch_shapes=[pltpu.VMEM((d, d), BF16), pltpu.VMEM((d, d), BF16), pltpu.VMEM((d, d), BF16)],
    )
    return pl.pallas_call(
        _expert_body,
        grid_spec=grid_spec,
        out_shape=jax.ShapeDtypeStruct((n_blk * MOE_ROWS, d), F32),
        compiler_params=_params(("arbitrary",)),
        name="experts",
    )(blk_e, n_used, xb, w_up, bg, bl, w_down, bd, perm)


def _combine_body(pos_ref, x_ref, gate_ref, y_hbm, g_ref, b_ref, o_ref, ybuf, sem):
    i = pl.program_id(0)
    nsteps = pl.num_programs(0)
    slot = lax.rem(i, 2)
    tm = COMBINE_ROWS

    def row_copy(src, slot_, k, r):
        return pltpu.make_async_copy(y_hbm.at[pl.ds(src, 1)], ybuf.at[slot_, k, pl.ds(r, 1)], sem.at[slot_])

    def issue(step, slot_):
        base = step * (tm * TOP_K)

        def body(g, c):
            for u in range(2):
                r = 2 * g + u
                for k in range(TOP_K):
                    row_copy(pos_ref[base + r * TOP_K + k], slot_, k, r).start()
            return c
        lax.fori_loop(0, tm // 2, body, 0)

    def wait(slot_):
        for k in range(TOP_K):
            pltpu.make_async_copy(y_hbm.at[pl.ds(0, tm)], ybuf.at[slot_, k], sem.at[slot_]).wait()

    @pl.when(i == 0)
    def _():
        issue(0, 0)

    @pl.when(i + 1 < nsteps)
    def _():
        issue(i + 1, 1 - slot)

    wait(slot)
    gates = gate_ref[...]
    m = gates[:, 0:1] * ybuf[slot, 0]
    for k in range(1, TOP_K):
        m = m + gates[:, k:k + 1] * ybuf[slot, k]
    o_ref[...] = _layer_norm(DN_ALPHA * x_ref[...] + m, g_ref[...], b_ref[...])


def _combine(x, gates, pos_flat, yb, g, b):
    n, d = x.shape
    tm = COMBINE_ROWS
    row = lambda i, p: (i, 0)
    fixed = lambda i, p: (0, 0)
    grid_spec = pltpu.PrefetchScalarGridSpec(
        num_scalar_prefetch=1,
        grid=(n // tm,),
        in_specs=[pl.BlockSpec((tm, d), row), pl.BlockSpec((tm, LANES), row),
                  pl.BlockSpec(memory_space=pl.ANY),
                  pl.BlockSpec((1, d), fixed), pl.BlockSpec((1, d), fixed)],
        out_specs=pl.BlockSpec((tm, d), row),
        scratch_shapes=[pltpu.VMEM((2, TOP_K, tm, d), F32), pltpu.SemaphoreType.DMA((2,))],
    )
    return pl.pallas_call(
        _combine_body,
        grid_spec=grid_spec,
        out_shape=jax.ShapeDtypeStruct((n, d), F32),
        compiler_params=_params(("arbitrary",)),
        name="combine",
    )(pos_flat, x, gates, yb, g.reshape(1, d), b.reshape(1, d))


def _moe(x, router_w, router_b, w_up, bg, bl, w_down, bd, ln_g, ln_b):
    n, d = x.shape
    idx_pad, gates, rank_pad, cnt = _router(x, router_w, router_b)
    nk = n * TOP_K
    counts = cnt[0].astype(I32)
    padded = (counts + MOE_ROWS - 1) // MOE_ROWS * MOE_ROWS
    ends = jnp.cumsum(padded)
    pad_start = ends - padded
    idx4 = idx_pad[:, :TOP_K]
    start4 = jnp.sum(jnp.where(idx4[..., None] == jnp.arange(N_EXPERTS), pad_start, 0), axis=-1)
    pos_flat = (start4 + rank_pad[:, :TOP_K]).reshape(nk).astype(I32)
    n_blk = -(-nk // MOE_ROWS) + N_EXPERTS
    blk_first = jnp.arange(n_blk, dtype=I32) * MOE_ROWS
    blk_e = jnp.minimum(jnp.sum(ends[None, :] <= blk_first[:, None], axis=1), N_EXPERTS - 1).astype(I32)
    n_used = (ends[-1:] // MOE_ROWS).astype(I32)
    xb = _dispatch(x, pos_flat, n_blk * MOE_ROWS)
    yb = _experts(xb, blk_e, n_used, w_up, bg, bl, w_down, bd)
    return _combine(x, gates, pos_flat, yb, ln_g, ln_b)


def _hgrn_cumsum_matrix(c):
    n_lv = int(math.log2(c))
    assert 1 << n_lv == c
    mats = []
    t = np.arange(c)
    for lv in range(n_lv):
        m = 1 << lv
        mat = np.zeros((c, c), np.float32)
        seg0 = (t // m) * m
        for r in range(c):
            if (r // m) % 2 == 1:
                mat[r, seg0[r]:r + 1] = 1.0
            else:
                mat[r, r + 1:seg0[r] + m] = 1.0
        mats.append(mat)
    mats.append(np.tril(np.ones((c, c), np.float32)))
    mats.append(np.triu(np.ones((c, c), np.float32), 1))
    return np.concatenate(mats, axis=0), n_lv


def _hgrn_body(c, n_lv, q_ref, f_ref, i_ref, g_ref, s0_ref, low_ref, nw_ref, l_ref,
               o_ref, sout_ref, st_scr, o_scr):
    step = pl.program_id(1)
    n_steps = pl.num_programs(1)

    @pl.when(step == 0)
    def _():
        for h in range(A_HEADS):
            st_scr[h] = s0_ref[0, h].T

    low = low_ref[...]
    f = low + (1.0 - low) * jax.nn.sigmoid(f_ref[0])
    lg = jnp.log(f)
    kk = 1.0 - f
    qz = q_ref[0]
    qq = qz * jax.nn.sigmoid(qz)
    vv = i_ref[0]

    hi = lg.astype(BF16)
    r1 = lg - hi.astype(F32)
    mid = r1.astype(BF16)
    lo = (r1 - mid.astype(F32)).astype(BF16)
    lmat = l_ref[...]
    z = (jnp.dot(lmat, hi, preferred_element_type=F32) + jnp.dot(lmat, mid, preferred_element_type=F32)
         + jnp.dot(lmat, lo, preferred_element_type=F32))

    row = lax.broadcasted_iota(I32, (c, 1), 0)
    rt = lax.broadcasted_iota(I32, (c, c), 0)
    rs = lax.broadcasted_iota(I32, (c, c), 1)
    qps, kps, same = [qq.astype(BF16)], [kk.astype(BF16)], [rt == rs]
    for lv in range(n_lv):
        e = jnp.exp(z[lv * c:(lv + 1) * c])
        upper = ((row >> lv) & 1) == 1
        qps.append(jnp.where(upper, qq * e, 0.0).astype(BF16))
        kps.append(jnp.where(upper, 0.0, kk * e).astype(BF16))
        same.append((rt >> (lv + 1)) == (rs >> (lv + 1)))
    b = z[n_lv * c:(n_lv + 1) * c]
    qb = (qq * jnp.exp(b)).astype(BF16)
    kc = (kk * jnp.exp(z[(n_lv + 1) * c:(n_lv + 2) * c])).astype(BF16)
    e_last = jnp.exp(b[c - 1:c, :])
    vb = vv.astype(BF16)

    for h in range(A_HEADS):
        sl = slice(h * A_DK, (h + 1) * A_DK)
        a = jnp.zeros((c, c), F32)
        for qp, kp, sm in zip(qps, kps, same):
            a = a + jnp.where(sm, _dot_nt(qp[:, sl], kp[:, sl]), 0.0)
        st = st_scr[h]
        o_h = _dot_nt(qb[:, sl], st.astype(BF16)) + jnp.dot(a.astype(BF16), vb[:, sl],
                                                             preferred_element_type=F32)
        o_scr[:, sl] = o_h
        upd = lax.dot_general(vb[:, sl], kc[:, sl], (((0,), (0,)), ((), ())), preferred_element_type=F32)
        st_scr[h] = st * e_last[:, sl] + upd

    o = o_scr[...]
    o = o * lax.rsqrt(jnp.mean(o * o, axis=-1, keepdims=True) + RMS_EPS) * nw_ref[...]
    gz = g_ref[0]
    o_ref[0] = o * (gz * jax.nn.sigmoid(gz))

    @pl.when(step == n_steps - 1)
    def _():
        for h in range(A_HEADS):
            sout_ref[0, h] = st_scr[h].T


def _hgrn(proj, s0, lower, norm_w, c):
    bsz, t, _ = proj.shape
    d = D_MODEL
    lmat, n_lv = _hgrn_cumsum_matrix(c)
    lmat = jnp.asarray(lmat, BF16)
    blk = lambda col: pl.BlockSpec((1, c, d), lambda b, s, col=col: (b, s, col))
    st_spec = pl.BlockSpec((1, A_HEADS, A_DK, A_DK), lambda b, s: (b, 0, 0, 0))
    vec = pl.BlockSpec((1, d), lambda b, s: (0, 0))
    return pl.pallas_call(
        functools.partial(_hgrn_body, c, n_lv),
        grid=(bsz, t // c),
        in_specs=[blk(0), blk(1), blk(2), blk(3), st_spec, vec, vec,
                  pl.BlockSpec(lmat.shape, lambda b, s: (0, 0))],
        out_specs=[pl.BlockSpec((1, c, d), lambda b, s: (b, s, 0)), st_spec],
        out_shape=[jax.ShapeDtypeStruct((bsz, t, d), F32), jax.ShapeDtypeStruct(s0.shape, F32)],
        scratch_shapes=[pltpu.VMEM((A_HEADS, A_DK, A_DK), F32), pltpu.VMEM((c, d), F32)],
        compiler_params=_params(("arbitrary", "arbitrary")),
        name=f"hgrn_c{c}",
    )(proj, proj, proj, proj, s0, lower.reshape(1, d), norm_w.reshape(1, d), lmat)


def _bias_by_distance(rel_bias, n):
    dist = jnp.arange(n, dtype=I32)
    max_exact = N_BUCKETS // 2
    scaled = jnp.log(jnp.maximum(dist, max_exact).astype(F32) / max_exact) / math.log(MAX_DISTANCE / max_exact)
    large = jnp.minimum(max_exact + (scaled * (N_BUCKETS - max_exact)).astype(I32), N_BUCKETS - 1)
    bucket = jnp.where(dist < max_exact, dist, large)
    onehot = (bucket[:, None] == jnp.arange(N_BUCKETS, dtype=I32)[None, :]).astype(F32)
    return jnp.dot(onehot, rel_bias.astype(F32), precision=HIGHEST).T


def _toeplitz(w, n):
    h = w.shape[0]
    wext = jnp.concatenate([w, jnp.zeros((h, 1), w.dtype)], axis=1)
    m = jnp.tile(wext, (1, n))[:, :n * (2 * n - 1)].reshape(h, n, 2 * n - 1)
    return m[:, :, n - 1:]


def _top3_rows(g, rowb, n_rows):
    sel = jnp.zeros(g.shape, F32)
    for _ in range(MOBA_TOPK):
        m = jnp.max(g, axis=0, keepdims=True)
        ix = jnp.min(jnp.where(g == m, rowb, n_rows), axis=0, keepdims=True)
        pick = rowb == ix
        sel = jnp.where(pick & (m > -jnp.inf), 1.0, sel)
        g = jnp.where(pick, -jnp.inf, g)
    return sel


def _moba_p_body(n_blk, farb_ref, q_ref, k_ref, vt_ref, km_ref, bd_ref, bp_ref, o_ref,
                 row_scr, s_scr, sd_scr, mx_scr, l_scr, acc_scr):
    hp = pl.program_id(1)
    i = pl.program_id(2)
    blk = MOBA_BLOCK

    q = q_ref[0] * (B_HEAD_DIM ** -0.5)
    lane = lax.broadcasted_iota(I32, q.shape, 1)
    qh = [jnp.where(lane < B_HEAD_DIM, q, 0.0), jnp.where(lane >= B_HEAD_DIM, q, 0.0)]
    qhb = [(x * LOG2E).astype(BF16) for x in qh]
    km = km_ref[0]
    rowb = lax.broadcasted_iota(I32, (n_blk, blk), 0)
    for a in range(2):
        g = _dot_nt(km, qh[a], precision=HIGHEST)
        g = jnp.where(rowb < i, g, -jnp.inf)
        row_scr[a] = jnp.where(_top3_rows(g, rowb, n_blk) > 0.0, 0.0, NEG_INF)

    def k_block(j):
        return k_ref[0, pl.ds(pl.multiple_of(j * blk, blk), blk), :].astype(BF16)

    def col_max8(s):
        return jnp.max(s.reshape(blk // 8, 8, blk), axis=0)

    kd = k_block(i)
    for a in range(2):
        s = _dot_nt(kd, qhb[a]) + bd_ref[a]
        sd_scr[a] = s
        mx_scr[a] = col_max8(s)

    n_far = jnp.maximum(i - 1, 0)

    def far_logits(t, c):
        for u in range(2):
            j = 2 * t + u
            kb = k_block(j)
            live = j < n_far
            for a in range(2):
                row = jnp.where(live, row_scr[a, pl.ds(j, 1), :] + farb_ref[2 * hp + a], NEG_INF)
                s = _dot_nt(kb, qhb[a]) + row
                s_scr[a, j] = s
                mx_scr[a] = jnp.maximum(mx_scr[a], col_max8(s))
        return c
    lax.fori_loop(0, (n_far + 1) // 2, far_logits, 0)

    @pl.when(i >= 1)
    def _():
        kb = k_block(i - 1)
        for a in range(2):
            s = _dot_nt(kb, qhb[a]) + bp_ref[a] + row_scr[a, pl.ds(i - 1, 1), :]
            s_scr[a, i - 1] = s
            mx_scr[a] = jnp.maximum(mx_scr[a], col_max8(s))

    m = [jnp.max(mx_scr[a], axis=0, keepdims=True) for a in range(2)]
    vtd = vt_ref[0, i]
    for a in range(2):
        p = jnp.exp2(sd_scr[a] - m[a])
        l_scr[a] = jnp.sum(p, axis=0, keepdims=True)
        acc_scr[a] = jnp.dot(vtd, p.astype(BF16), preferred_element_type=F32)

    def past_values(t, c):
        lsum = [jnp.zeros((1, blk), F32), jnp.zeros((1, blk), F32)]
        pv = [jnp.zeros((LANES, blk), F32), jnp.zeros((LANES, blk), F32)]
        for u in range(2):
            j = 2 * t + u
            live = j < i
            jc = jnp.where(live, j, 0)
            vtb = vt_ref[0, jc]
            for a in range(2):
                p = jnp.where(live, jnp.exp2(s_scr[a, jc] - m[a]), 0.0)
                lsum[a] = lsum[a] + jnp.sum(p, axis=0, keepdims=True)
                pv[a] = pv[a] + jnp.dot(vtb, p.astype(BF16), preferred_element_type=F32)
        for a in range(2):
            l_scr[a] = l_scr[a] + lsum[a]
            acc_scr[a] = acc_scr[a] + pv[a]
        return c
    lax.fori_loop(0, (i + 1) // 2, past_values, 0)

    out0 = acc_scr[0] / l_scr[0]
    out1 = acc_scr[1] / l_scr[1]
    sub = lax.broadcasted_iota(I32, (LANES, blk), 0)
    o_ref[0] = jnp.where(sub < B_HEAD_DIM, out0, out1).T


def _moba_prompt(q, kv, vt, k_mean, bias_tbl):
    bsz, t, d = q.shape
    blk = MOBA_BLOCK
    n_blk = t // blk
    n_hp = B_HEADS // 2
    bias2 = bias_tbl * LOG2E
    bias_diag = _toeplitz(jnp.concatenate([jnp.full((B_HEADS, blk - 1), NEG_INF, F32), bias2[:, :blk]], axis=1), blk)
    bias_prev = _toeplitz(bias2[:, 1:2 * blk], blk)
    far_bias = bias2[:, 2 * blk - 1]
    return pl.pallas_call(
        functools.partial(_moba_p_body, n_blk),
        grid=(bsz, n_hp, n_blk),
        in_specs=[pl.BlockSpec(memory_space=pltpu.SMEM),
                  pl.BlockSpec((1, blk, LANES), lambda b, h, i: (b, i, h)),
                  pl.BlockSpec((1, t, LANES), lambda b, h, i: (b, 0, h)),
                  pl.BlockSpec((1, n_blk, LANES, blk), lambda b, h, i: (b, 0, h, 0)),
                  pl.BlockSpec((1, n_blk, LANES), lambda b, h, i: (b, 0, h)),
                  pl.BlockSpec((2, blk, blk), lambda b, h, i: (h, 0, 0)),
                  pl.BlockSpec((2, blk, blk), lambda b, h, i: (h, 0, 0))],
        out_specs=pl.BlockSpec((1, blk, LANES), lambda b, h, i: (b, i, h)),
        out_shape=jax.ShapeDtypeStruct((bsz, t, d), F32),
        scratch_shapes=[pltpu.VMEM((2, n_blk, blk), F32),
                        pltpu.VMEM((2, n_blk, blk, blk), F32), pltpu.VMEM((2, blk, blk), F32),
                        pltpu.VMEM((2, 8, blk), F32), pltpu.VMEM((2, 1, blk), F32),
                        pltpu.VMEM((2, LANES, blk), F32)],
        compiler_params=_params(("arbitrary", "arbitrary", "arbitrary")),
        name="moba_prompt",
    )(far_bias, q, kv, vt, k_mean, bias_diag, bias_prev)


def _pack_body(pt_ref, *refs):
    del pt_ref
    g = MOBA_BLOCK // LANES
    kc_refs, vc_refs = refs[:g], refs[g:2 * g]
    kp_ref, vp_ref, km_ref = refs[2 * g:]
    page = kc_refs[0].shape[1] // B_HEADS

    def flat(ref):
        return jnp.concatenate([ref[0, pl.ds(h, page, stride=B_HEADS), :] for h in range(B_HEADS)], axis=1)

    tot = jnp.zeros((1, D_MODEL), F32)
    for u in range(g):
        k = flat(kc_refs[u])
        tot = tot + jnp.sum(k, axis=0, keepdims=True)
        kp_ref[0, u * page:(u + 1) * page, :] = k.astype(BF16)
        vp_ref[0, u * page:(u + 1) * page, :] = flat(vc_refs[u]).astype(BF16)
    km_ref[0, 0] = tot * (1.0 / MOBA_BLOCK)


def _cache_pack(cache_k, cache_v, page_table):
    s, n_pages = page_table.shape
    n_phys, page, hh, dh = cache_k.shape
    d = hh * dh
    assert page == LANES and MOBA_BLOCK % page == 0
    g = MOBA_BLOCK // page
    n_blk = n_pages // g
    cache_k = cache_k.reshape(n_phys, page * hh, dh)
    cache_v = cache_v.reshape(n_phys, page * hh, dh)
    maps = [lambda b, n, pt, u=u: (pt[b * n_pages + g * n + u], 0, 0) for u in range(g)]
    out_rows = pl.BlockSpec((1, MOBA_BLOCK, d), lambda b, n, pt: (b, n, 0))
    grid_spec = pltpu.PrefetchScalarGridSpec(
        num_scalar_prefetch=1,
        grid=(s, n_blk),
        in_specs=[pl.BlockSpec((1, page * hh, dh), m) for m in maps] * 2,
        out_specs=[out_rows, out_rows, pl.BlockSpec((1, 1, 1, d), lambda b, n, pt: (b, n, 0, 0))],
    )
    kp, vp, km = pl.pallas_call(
        _pack_body,
        grid_spec=grid_spec,
        out_shape=[jax.ShapeDtypeStruct((s, n_pages * page, d), BF16),
                   jax.ShapeDtypeStruct((s, n_pages * page, d), BF16),
                   jax.ShapeDtypeStruct((s, n_blk, 1, d), F32)],
        compiler_params=_params(("arbitrary", "arbitrary")),
        name="cache_pack",
    )(page_table.reshape(-1).astype(I32), *([cache_k] * g), *([cache_v] * g))
    return kp, vp, km.reshape(s, n_blk, d)


def _moba_s_body(n_pages, t_new, qbd_ref, k_ref, v_ref, km_ref, kn_ref, vn_ref, blast_ref, bnew_ref, farb_ref,
                 o_ref, s_scr, sel_scr, pn_scr, vn_scr, acc_scr, qb_scr):
    g = SAMPLE_PAGES_PER_STEP
    p = pl.program_id(1)
    page = LANES
    pages_per_blk = MOBA_BLOCK // page
    n_blk = n_pages // pages_per_blk
    n_steps = n_pages // g

    @pl.when(p == 0)
    def _():
        qb_scr[...] = qbd_ref[0].astype(BF16)

    @pl.when(p < n_steps)
    def _():
        s = jnp.dot(k_ref[0], qb_scr[...], preferred_element_type=F32)
        s_scr[pl.ds(p * g, g)] = s.reshape(g, page, LANES)

    @pl.when(p == n_steps - 1)
    def _():
        gate = jnp.dot(km_ref[0], qbd_ref[0], precision=HIGHEST, preferred_element_type=F32)
        rowb = lax.broadcasted_iota(I32, gate.shape, 0)
        sel_scr[...] = _top3_rows(gate, rowb, n_blk)

        sn = jnp.dot(kn_ref[0].astype(BF16), qb_scr[...], preferred_element_type=F32) + bnew_ref[...]
        kj = lax.broadcasted_iota(I32, sn.shape, 0)
        qi = lax.rem(lax.broadcasted_iota(I32, sn.shape, 1), t_new)
        sn = jnp.where(kj <= qi, sn, NEG_INF)
        farb = farb_ref[...]

        def pass_max(pg, m):
            bias = jnp.where(pg == n_pages - 1, blast_ref[...], farb)
            keep = sel_scr[pl.ds(pg // pages_per_blk, 1), :] > 0.0
            s = jnp.where(keep, s_scr[pg] + bias, NEG_INF)
            s_scr[pg] = s
            return jnp.maximum(m, jnp.max(s, axis=0, keepdims=True))
        m = lax.fori_loop(0, n_pages, pass_max, jnp.max(sn, axis=0, keepdims=True))

        def pass_exp(pg, l):
            e = jnp.exp(s_scr[pg] - m)
            s_scr[pg] = e
            return l + jnp.sum(e, axis=0, keepdims=True)
        en = jnp.exp(sn - m)
        l = lax.fori_loop(0, n_pages, pass_exp, jnp.sum(en, axis=0, keepdims=True))
        inv = 1.0 / l

        def pass_norm(pg, c):
            s_scr[pg] = s_scr[pg] * inv
            return c
        lax.fori_loop(0, n_pages, pass_norm, 0)
        pn_scr[...] = jnp.zeros(pn_scr.shape, F32)
        pn_scr[0:t_new, :] = en * inv
        vn_scr[...] = jnp.zeros(vn_scr.shape, F32)
        vn_scr[0:t_new, :] = vn_ref[0]
        acc_scr[...] = jnp.zeros(acc_scr.shape, F32)

    @pl.when(p >= n_steps)
    def _():
        pg0 = (p - n_steps) * g
        pt = jnp.concatenate([s_scr[pg0 + u].T for u in range(g)], axis=1).astype(BF16)
        acc_scr[...] = acc_scr[...] + jnp.dot(pt, v_ref[0], preferred_element_type=F32)

    @pl.when(p == 2 * n_steps - 1)
    def _():
        acc = acc_scr[...] + jnp.dot(pn_scr[...].T.astype(BF16), vn_scr[...].astype(BF16),
                                     preferred_element_type=F32)
        rh = lax.broadcasted_iota(I32, acc.shape, 0) // t_new
        ch = lax.broadcasted_iota(I32, acc.shape, 1) // B_HEAD_DIM
        acc = jnp.where(rh == ch, acc, 0.0)
        o_ref[0] = jnp.sum(acc.reshape(B_HEADS, t_new, D_MODEL), axis=0)


def _moba_sample(q, kv, k_past, v_past, k_mean, bias_tbl):
    s, t_new, d = q.shape
    page = LANES
    n_pages = k_past.shape[1] // page
    assert B_HEADS * t_new == LANES and k_past.shape[1] % MOBA_BLOCK == 0
    n_blk = n_pages // (MOBA_BLOCK // page)
    scale = B_HEAD_DIM ** -0.5
    qh = (q * scale).reshape(s, t_new, B_HEADS, B_HEAD_DIM).transpose(0, 2, 3, 1)
    eye = jnp.eye(B_HEADS, dtype=F32)
    qbd = (qh[:, :, :, None, :] * eye[None, :, None, :, None]).reshape(s, d, LANES)
    bias_last = jnp.stack([jnp.flip(bias_tbl[:, 1 + i:1 + i + page], axis=1) for i in range(t_new)], axis=1)
    bias_last = bias_last.transpose(2, 0, 1).reshape(page, LANES)
    near = np.maximum(np.arange(t_new)[None, :] - np.arange(t_new)[:, None], 0)
    bias_new = jnp.stack([jnp.stack([bias_tbl[:, near[j, i]] for i in range(t_new)], axis=1)
                          for j in range(t_new)], axis=0).reshape(t_new, LANES)
    far_bias = jnp.repeat(bias_tbl[:, 2 * MOBA_BLOCK - 1], t_new).reshape(1, LANES)
    k_new = kv[..., :d]
    v_new = kv[..., d:]
    g = min(SAMPLE_PAGES_PER_STEP, n_pages)
    assert n_pages % g == 0
    n_steps = n_pages // g
    rows = g * page
    seq = lambda b, p: (b, 0, 0)
    fixed = lambda b, p: (0, 0)
    return pl.pallas_call(
        functools.partial(_moba_s_body, n_pages, t_new),
        grid=(s, 2 * n_steps),
        in_specs=[pl.BlockSpec((1, d, LANES), seq),
                  pl.BlockSpec((1, rows, d), lambda b, p: (b, jnp.minimum(p, n_steps - 1), 0)),
                  pl.BlockSpec((1, rows, d), lambda b, p: (b, jnp.maximum(p - n_steps, 0), 0)),
                  pl.BlockSpec((1, n_blk, d), seq),
                  pl.BlockSpec((1, t_new, d), seq), pl.BlockSpec((1, t_new, d), seq),
                  pl.BlockSpec((page, LANES), fixed), pl.BlockSpec((t_new, LANES), fixed),
                  pl.BlockSpec((1, LANES), fixed)],
        out_specs=pl.BlockSpec((1, t_new, d), seq),
        out_shape=jax.ShapeDtypeStruct((s, t_new, d), F32),
        scratch_shapes=[pltpu.VMEM((n_pages, page, LANES), F32),
                        pltpu.VMEM((n_blk, LANES), F32), pltpu.VMEM((page, LANES), F32),
                        pltpu.VMEM((page, d), F32), pltpu.VMEM((LANES, d), F32),
                        pltpu.VMEM((d, LANES), BF16)],
        compiler_params=_params(("arbitrary", "arbitrary")),
        name="moba_sample",
    )(qbd, k_past, v_past, k_mean, k_new, v_new, bias_last, bias_new, far_bias)


def kernel(x_prompt, x_sample, cache_k, cache_v, state_hgrn, page_table, a_w_in, a_lb, a_norm, a_w_out,
           kv_w, b_w_q, b_w_o, rel_bias, router_w, router_b, w_up, b_up, w_down, b_down, ln_g, ln_b):
    bp, tp, d = x_prompt.shape
    bs, ts, _ = x_sample.shape
    n_p = bp * tp
    n_s = bs * ts
    x = jnp.concatenate([x_prompt.reshape(n_p, d), x_sample.reshape(n_s, d)], axis=0)

    p_lb = jax.nn.softmax(a_lb.astype(F32), axis=0)
    lower = jnp.cumsum(p_lb, axis=0) - p_lb[0]
    bias_tbl = _bias_by_distance(rel_bias, 2 * MOBA_BLOCK)
    zero_state = jnp.zeros((bp,) + state_hgrn.shape[2:], F32)
    hgrn_chunk = 64 if tp % 64 == 0 else tp

    states_p, states_s = [], []
    kv = None
    for l in range(DEPTH):
        if l < N_A_LAYERS:
            proj = _proj(x, a_w_in[l].astype(BF16))
            o_p, s_p = _hgrn(proj[:n_p].reshape(bp, tp, 4 * d), zero_state, lower[l], a_norm[l], hgrn_chunk)
            o_s, s_s = _hgrn(proj[n_p:].reshape(bs, ts, 4 * d), state_hgrn[l], lower[l], a_norm[l], ts)
            states_p.append(s_p)
            states_s.append(s_s)
            mix = jnp.concatenate([o_p.reshape(n_p, d), o_s.reshape(n_s, d)], axis=0)
            x = _mm_res_ln(x, mix, a_w_out[l].astype(BF16), ln_g[l, 0], ln_b[l, 0])
        else:
            j = l - N_A_LAYERS
            if kv is None:
                kv, vt_all, km_all = _kv_proj(x, kv_w.astype(BF16))
                kv_p = kv[:n_p].reshape(bp, tp, 2 * d)
                kv_s = kv[n_p:].reshape(bs, ts, 2 * d)
                n_kb = tp // MOBA_BLOCK
                vt_p = vt_all[:bp * n_kb].reshape(bp, n_kb, d, MOBA_BLOCK)
                km_p = km_all[:bp * n_kb].reshape(bp, n_kb, d)
                k_past, v_past, k_mean_s = _cache_pack(cache_k, cache_v, page_table)
            q = _proj(x, b_w_q[j].astype(BF16))
            o_p = _moba_prompt(q[:n_p].reshape(bp, tp, d), kv_p, vt_p, km_p, bias_tbl)
            o_s = _moba_sample(q[n_p:].reshape(bs, ts, d), kv_s, k_past, v_past, k_mean_s, bias_tbl)
            mix = jnp.concatenate([o_p.reshape(n_p, d), o_s.reshape(n_s, d)], axis=0)
            x = _mm_res_ln(x, mix, b_w_o[j].astype(BF16), ln_g[l, 0], ln_b[l, 0])
        bg = b_up[l, :, 0::2].reshape(N_EXPERTS, 1, d)
        bl = b_up[l, :, 1::2].reshape(N_EXPERTS, 1, d)
        x = _moe(x, router_w[l], router_b[l], w_up[l], bg, bl, w_down[l],
                 b_down[l].reshape(N_EXPERTS, 1, d), ln_g[l, 1], ln_b[l, 1])

    y_p = x[:n_p].reshape(bp, tp, d)
    y_s = x[n_p:].reshape(bs, ts, d)
    heads = (B_HEADS, B_HEAD_DIM)
    return (y_p, y_s,
            kv_p[..., :d].reshape(bp, tp, *heads), kv_p[..., d:].reshape(bp, tp, *heads),
            jnp.stack(states_p),
            kv_s[..., :d].reshape(bs, ts, *heads), kv_s[..., d:].reshape(bs, ts, *heads),
            jnp.stack(states_s))
```

```python
import functools
import math

import jax
import jax.numpy as jnp
import numpy as np
from jax import lax
from jax.experimental import pallas as pl
from jax.experimental.pallas import tpu as pltpu

F32 = jnp.float32
BF16 = jnp.bfloat16
I32 = jnp.int32
HIGHEST = lax.Precision.HIGHEST

D_MODEL = 1024
DEPTH = 4
N_A_LAYERS = 2
A_HEADS = 8
A_DK = 128
B_HEADS = 16
B_HEAD_DIM = 64
MOBA_BLOCK = 256
MOBA_TOPK = 3
N_BUCKETS = 32
MAX_DISTANCE = 128
N_EXPERTS = 32
TOP_K = 4
SWIGLU_LIMIT = 7.0
SWIGLU_ALPHA = 1.702
DN_ALPHA = (2 * DEPTH) ** 0.25
LN_EPS = 1e-5
RMS_EPS = 1e-6
NEG_INF = -1e30
LOG2E = math.log2(math.e)

LANES = 128
VMEM_LIMIT = 56 * 1024 * 1024
MOE_ROWS = 256
COMBINE_ROWS = 128
DISPATCH_ROWS = 128
SAMPLE_PAGES_PER_STEP = 8


def _params(sem, vmem=VMEM_LIMIT):
    return pltpu.CompilerParams(dimension_semantics=sem, vmem_limit_bytes=vmem)


def _row_tile(n, cap):
    best = None
    for t in range(8, min(n, cap) + 1, 8):
        if n % t == 0:
            best = t
    assert best is not None, n
    return best


def _dot_nt(a, b, precision=None):
    return lax.dot_general(a, b, (((1,), (1,)), ((), ())), precision=precision,
                           preferred_element_type=F32)


def _layer_norm(y, g, b):
    mu = jnp.mean(y, axis=-1, keepdims=True)
    yc = y - mu
    var = jnp.mean(yc * yc, axis=-1, keepdims=True)
    return yc * lax.rsqrt(var + LN_EPS) * g + b


def _proj_body(x_ref, w_ref, o_ref):
    o_ref[...] = jnp.dot(x_ref[...].astype(BF16), w_ref[...], preferred_element_type=F32)


def _proj(x, w_bf16):
    n, k = x.shape
    m = w_bf16.shape[1]
    tn = 1024
    tm = _row_tile(n, 640)
    return pl.pallas_call(
        _proj_body,
        grid=(m // tn, n // tm),
        in_specs=[pl.BlockSpec((tm, k), lambda j, i: (i, 0)),
                  pl.BlockSpec((k, tn), lambda j, i: (0, j))],
        out_specs=pl.BlockSpec((tm, tn), lambda j, i: (i, j)),
        out_shape=jax.ShapeDtypeStruct((n, m), F32),
        compiler_params=_params(("arbitrary", "arbitrary")),
        name="proj",
    )(x, w_bf16)


def _kv_proj_body(x_ref, w_ref, kv_ref, vt_ref, km_ref):
    y = jnp.dot(x_ref[...].astype(BF16), w_ref[...], preferred_element_type=F32)
    kv_ref[...] = y
    km_ref[0] = jnp.mean(y[:, :D_MODEL], axis=0, keepdims=True)
    vt_ref[0] = y[:, D_MODEL:].T.astype(BF16)


def _kv_proj(x, w_bf16):
    n, k = x.shape
    d = D_MODEL
    tm = MOBA_BLOCK
    assert n % tm == 0
    return pl.pallas_call(
        _kv_proj_body,
        grid=(n // tm,),
        in_specs=[pl.BlockSpec((tm, k), lambda i: (i, 0)), pl.BlockSpec((k, 2 * d), lambda i: (0, 0))],
        out_specs=[pl.BlockSpec((tm, 2 * d), lambda i: (i, 0)),
                   pl.BlockSpec((1, d, tm), lambda i: (i, 0, 0)),
                   pl.BlockSpec((1, 1, d), lambda i: (i, 0, 0))],
        out_shape=[jax.ShapeDtypeStruct((n, 2 * d), F32),
                   jax.ShapeDtypeStruct((n // tm, d, tm), BF16),
                   jax.ShapeDtypeStruct((n // tm, 1, d), F32)],
        compiler_params=_params(("arbitrary",)),
        name="kv_proj",
    )(x, w_bf16)


def _mm_res_ln_body(x_ref, a_ref, w_ref, g_ref, b_ref, o_ref):
    h = jnp.dot(a_ref[...].astype(BF16), w_ref[...], preferred_element_type=F32)
    o_ref[...] = _layer_norm(DN_ALPHA * x_ref[...] + h, g_ref[...], b_ref[...])


def _mm_res_ln(x, a, w_bf16, g, b):
    n, d = x.shape
    tm = _row_tile(n, 640)
    row = lambda i: (i, 0)
    fixed = lambda i: (0, 0)
    return pl.pallas_call(
        _mm_res_ln_body,
        grid=(n // tm,),
        in_specs=[pl.BlockSpec((tm, d), row), pl.BlockSpec((tm, d), row),
                  pl.BlockSpec((d, d), fixed), pl.BlockSpec((1, d), fixed), pl.BlockSpec((1, d), fixed)],
        out_specs=pl.BlockSpec((tm, d), row),
        out_shape=jax.ShapeDtypeStruct((n, d), F32),
        compiler_params=_params(("arbitrary",)),
        name="mm_res_ln",
    )(x, a, w_bf16, g.reshape(1, d), b.reshape(1, d))


def _router_body(x_ref, w_ref, b_ref, tri_ref, idx_ref, gate_ref, rank_ref, cnt_ref, carry_scr):
    @pl.when(pl.program_id(0) == 0)
    def _():
        carry_scr[...] = jnp.zeros(carry_scr.shape, F32)

    logits = jnp.dot(x_ref[...], w_ref[...], precision=HIGHEST, preferred_element_type=F32) + b_ref[...]
    tm = logits.shape[0]
    col = lax.broadcasted_iota(I32, logits.shape, 1)
    lane = lax.broadcasted_iota(I32, (tm, LANES), 1)
    vals, idxs = [], []
    cur = logits
    for _ in range(TOP_K):
        m = jnp.max(cur, axis=-1, keepdims=True)
        ix = jnp.min(jnp.where(cur == m, col, N_EXPERTS), axis=-1, keepdims=True)
        vals.append(m)
        idxs.append(ix)
        cur = jnp.where(col == ix, -jnp.inf, cur)
    es = [jnp.exp(v - vals[0]) for v in vals]
    tot = es[0] + es[1] + es[2] + es[3]
    member = jnp.zeros(logits.shape, F32)
    for k in range(TOP_K):
        member = member + jnp.where(col == idxs[k], 1.0, 0.0)
    before = carry_scr[...] + jnp.dot(tri_ref[...], member.astype(BF16), preferred_element_type=F32)
    carry_scr[...] = carry_scr[...] + jnp.sum(member, axis=0, keepdims=True)
    cnt_ref[...] = jnp.broadcast_to(carry_scr[...], cnt_ref.shape)

    idx_out = jnp.zeros((tm, LANES), I32)
    gate_out = jnp.zeros((tm, LANES), F32)
    rank_out = jnp.zeros((tm, LANES), I32)
    for k in range(TOP_K):
        rank_k = jnp.sum(jnp.where(col == idxs[k], before, 0.0), axis=-1, keepdims=True)
        idx_out = jnp.where(lane == k, idxs[k], idx_out)
        gate_out = jnp.where(lane == k, es[k] / tot, gate_out)
        rank_out = jnp.where(lane == k, rank_k.astype(I32), rank_out)
    idx_ref[...] = idx_out
    gate_ref[...] = gate_out
    rank_ref[...] = rank_out


def _router(x, w, b):
    n, d = x.shape
    tm = _row_tile(n, 640)
    row = lambda i: (i, 0)
    fixed = lambda i: (0, 0)
    tri = jnp.asarray(np.tril(np.ones((tm, tm), np.float32), -1), BF16)
    return pl.pallas_call(
        _router_body,
        grid=(n // tm,),
        in_specs=[pl.BlockSpec((tm, d), row), pl.BlockSpec((d, N_EXPERTS), fixed),
                  pl.BlockSpec((1, N_EXPERTS), fixed), pl.BlockSpec((tm, tm), fixed)],
        out_specs=[pl.BlockSpec((tm, LANES), row), pl.BlockSpec((tm, LANES), row),
                   pl.BlockSpec((tm, LANES), row), pl.BlockSpec((8, N_EXPERTS), fixed)],
        out_shape=[jax.ShapeDtypeStruct((n, LANES), I32), jax.ShapeDtypeStruct((n, LANES), F32),
                   jax.ShapeDtypeStruct((n, LANES), I32), jax.ShapeDtypeStruct((8, N_EXPERTS), F32)],
        scratch_shapes=[pltpu.VMEM((1, N_EXPERTS), F32)],
        compiler_params=_params(("arbitrary",)),
        name="router",
    )(x, w, b.reshape(1, N_EXPERTS), tri)


def _deinterleave_matrix():
    p = np.zeros((2 * LANES, 2 * LANES), np.float32)
    j = np.arange(LANES)
    p[2 * j, j] = 1.0
    p[2 * j + 1, LANES + j] = 1.0
    return p


def _dispatch_body(pos_ref, x_ref, xb_in, xb_hbm, buf, sem):
    del xb_in
    i = pl.program_id(0)
    nsteps = pl.num_programs(0)
    slot = lax.rem(i, 2)
    tm = DISPATCH_ROWS

    def wait(slot_):
        for _ in range(TOP_K):
            pltpu.make_async_copy(buf.at[slot_], xb_hbm.at[pl.ds(0, tm)], sem.at[slot_]).wait()

    @pl.when(i >= 2)
    def _():
        wait(slot)

    buf[slot] = x_ref[...]
    base = i * (tm * TOP_K)

    def body(g, c):
        for u in range(2):
            r = 2 * g + u
            for k in range(TOP_K):
                dst = pos_ref[base + r * TOP_K + k]
                pltpu.make_async_copy(buf.at[slot, pl.ds(r, 1)], xb_hbm.at[pl.ds(dst, 1)], sem.at[slot]).start()
        return c
    lax.fori_loop(0, tm // 2, body, 0)

    @pl.when(i == nsteps - 1)
    def _():
        wait(slot)

    @pl.when((i == nsteps - 1) & (i >= 1))
    def _():
        wait(1 - slot)


def _dispatch(x, pos_flat, n_rows):
    n, d = x.shape
    tm = DISPATCH_ROWS
    grid_spec = pltpu.PrefetchScalarGridSpec(
        num_scalar_prefetch=1,
        grid=(n // tm,),
        in_specs=[pl.BlockSpec((tm, d), lambda i, p: (i, 0)), pl.BlockSpec(memory_space=pl.ANY)],
        out_specs=pl.BlockSpec(memory_space=pl.ANY),
        scratch_shapes=[pltpu.VMEM((2, tm, d), F32), pltpu.SemaphoreType.DMA((2,))],
    )
    return pl.pallas_call(
        _dispatch_body,
        grid_spec=grid_spec,
        out_shape=jax.ShapeDtypeStruct((n_rows, d), F32),
        input_output_aliases={2: 0},
        compiler_params=_params(("arbitrary",)),
        name="dispatch",
    )(pos_flat, x, jnp.zeros((n_rows, d), F32))


def _expert_body(blk_e_ref, nblk_ref, x_ref, wu_ref, bg_ref, bl_ref, wd_ref, bd_ref, perm_ref, y_ref,
                 wg_scr, wl_scr, wd_scr):
    i = pl.program_id(0)
    nb = nblk_ref[0]
    changed = (i == 0) | (blk_e_ref[i] != blk_e_ref[jnp.maximum(i - 1, 0)])

    @pl.when((i < nb) & changed)
    def _():
        d = D_MODEL
        perm = perm_ref[...]
        for g in range(d // LANES):
            wblk = wu_ref[0, :, 2 * LANES * g:2 * LANES * (g + 1)].astype(BF16)
            t = jnp.dot(wblk, perm, preferred_element_type=F32)
            wg_scr[:, LANES * g:LANES * (g + 1)] = t[:, :LANES].astype(BF16)
            wl_scr[:, LANES * g:LANES * (g + 1)] = t[:, LANES:].astype(BF16)
        wd_scr[...] = wd_ref[0].astype(BF16)

    @pl.when(i < nb)
    def _():
        xb = x_ref[...].astype(BF16)
        hg = jnp.dot(xb, wg_scr[...], preferred_element_type=F32) + bg_ref[0]
        hl = jnp.dot(xb, wl_scr[...], preferred_element_type=F32) + bl_ref[0]
        glu = jnp.minimum(hg, SWIGLU_LIMIT)
        lin = jnp.clip(hl, -SWIGLU_LIMIT, SWIGLU_LIMIT)
        a = glu * jax.nn.sigmoid(SWIGLU_ALPHA * glu) * (lin + 1.0)
        y_ref[...] = jnp.dot(a.astype(BF16), wd_scr[...], preferred_element_type=F32) + bd_ref[0]

    @pl.when(i >= nb)
    def _():
        y_ref[...] = jnp.zeros(y_ref.shape, F32)


def _experts(xb, blk_e, n_used, layer, w_up, bg, bl, w_down, bd):
    d = xb.shape[1]
    n_blk = blk_e.shape[0]
    w_map = lambda i, be, nb: (be[i], 0, 0)
    lw_map = lambda i, be, nb: (layer, be[i], 0, 0)
    rows = lambda i, be, nb: (i, 0)
    perm = jnp.asarray(_deinterleave_matrix(), BF16)
    grid_spec = pltpu.PrefetchScalarGridSpec(
        num_scalar_prefetch=2,
        grid=(n_blk,),
        in_specs=[pl.BlockSpec((MOE_ROWS, d), rows),
                  pl.BlockSpec((None, 1, d, 2 * d), lw_map),
                  pl.BlockSpec((1, 1, d), w_map), pl.BlockSpec((1, 1, d), w_map),
                  pl.BlockSpec((None, 1, d, d), lw_map), pl.BlockSpec((1, 1, d), w_map),
                  pl.BlockSpec(perm.shape, lambda i, be, nb: (0, 0))],
        out_specs=pl.BlockSpec((MOE_ROWS, d), rows),
        scratch_shapes=[pltpu.VMEM((d, d), BF16), pltpu.VMEM((d, d), BF16), pltpu.VMEM((d, d), BF16)],
    )
    return pl.pallas_call(
        _expert_body,
        grid_spec=grid_spec,
        out_shape=jax.ShapeDtypeStruct((n_blk * MOE_ROWS, d), F32),
        compiler_params=_params(("arbitrary",)),
        name="experts",
    )(blk_e, n_used, xb, w_up, bg, bl, w_down, bd, perm)


def _combine_body(pos_ref, x_ref, gate_ref, y_hbm, g_ref, b_ref, o_ref, ybuf, sem):
    i = pl.program_id(0)
    nsteps = pl.num_programs(0)
    slot = lax.rem(i, 2)
    tm = COMBINE_ROWS

    def row_copy(src, slot_, k, r):
        return pltpu.make_async_copy(y_hbm.at[pl.ds(src, 1)], ybuf.at[slot_, k, pl.ds(r, 1)], sem.at[slot_])

    def issue(step, slot_):
        base = step * (tm * TOP_K)

        def body(g, c):
            for u in range(2):
                r = 2 * g + u
                for k in range(TOP_K):
                    row_copy(pos_ref[base + r * TOP_K + k], slot_, k, r).start()
            return c
        lax.fori_loop(0, tm // 2, body, 0)

    def wait(slot_):
        for k in range(TOP_K):
            pltpu.make_async_copy(y_hbm.at[pl.ds(0, tm)], ybuf.at[slot_, k], sem.at[slot_]).wait()

    @pl.when(i == 0)
    def _():
        issue(0, 0)

    @pl.when(i + 1 < nsteps)
    def _():
        issue(i + 1, 1 - slot)

    wait(slot)
    gates = gate_ref[...]
    m = gates[:, 0:1] * ybuf[slot, 0]
    for k in range(1, TOP_K):
        m = m + gates[:, k:k + 1] * ybuf[slot, k]
    o_ref[...] = _layer_norm(DN_ALPHA * x_ref[...] + m, g_ref[...], b_ref[...])


def _combine(x, gates, pos_flat, yb, g, b):
    n, d = x.shape
    tm = COMBINE_ROWS
    row = lambda i, p: (i, 0)
    fixed = lambda i, p: (0, 0)
    grid_spec = pltpu.PrefetchScalarGridSpec(
        num_scalar_prefetch=1,
        grid=(n // tm,),
        in_specs=[pl.BlockSpec((tm, d), row), pl.BlockSpec((tm, LANES), row),
                  pl.BlockSpec(memory_space=pl.ANY),
                  pl.BlockSpec((1, d), fixed), pl.BlockSpec((1, d), fixed)],
        out_specs=pl.BlockSpec((tm, d), row),
        scratch_shapes=[pltpu.VMEM((2, TOP_K, tm, d), F32), pltpu.SemaphoreType.DMA((2,))],
    )
    return pl.pallas_call(
        _combine_body,
        grid_spec=grid_spec,
        out_shape=jax.ShapeDtypeStruct((n, d), F32),
        compiler_params=_params(("arbitrary",)),
        name="combine",
    )(pos_flat, x, gates, yb, g.reshape(1, d), b.reshape(1, d))


def _moe(x, layer, router_w, router_b, w_up, bg, bl, w_down, bd, ln_g, ln_b):
    n, d = x.shape
    idx_pad, gates, rank_pad, cnt = _router(x, router_w, router_b)
    nk = n * TOP_K
    counts = cnt[0].astype(I32)
    padded = (counts + MOE_ROWS - 1) // MOE_ROWS * MOE_ROWS
    ends = jnp.cumsum(padded)
    pad_start = ends - padded
    idx4 = idx_pad[:, :TOP_K]
    start4 = jnp.sum(jnp.where(idx4[..., None] == jnp.arange(N_EXPERTS), pad_start, 0), axis=-1)
    pos_flat = (start4 + rank_pad[:, :TOP_K]).reshape(nk).astype(I32)
    n_blk = -(-nk // MOE_ROWS) + N_EXPERTS
    blk_first = jnp.arange(n_blk, dtype=I32) * MOE_ROWS
    blk_e = jnp.minimum(jnp.sum(ends[None, :] <= blk_first[:, None], axis=1), N_EXPERTS - 1).astype(I32)
    n_used = (ends[-1:] // MOE_ROWS).astype(I32)
    xb = _dispatch(x, pos_flat, n_blk * MOE_ROWS)
    yb = _experts(xb, blk_e, n_used, layer, w_up, bg, bl, w_down, bd)
    return _combine(x, gates, pos_flat, yb, ln_g, ln_b)


def _hgrn_cumsum_matrix(c):
    n_lv = int(math.log2(c))
    assert 1 << n_lv == c
    mats = []
    t = np.arange(c)
    for lv in range(n_lv):
        m = 1 << lv
        mat = np.zeros((c, c), np.float32)
        seg0 = (t // m) * m
        for r in range(c):
            if (r // m) % 2 == 1:
                mat[r, seg0[r]:r + 1] = 1.0
            else:
                mat[r, r + 1:seg0[r] + m] = 1.0
        mats.append(mat)
    mats.append(np.tril(np.ones((c, c), np.float32)))
    mats.append(np.triu(np.ones((c, c), np.float32), 1))
    return np.concatenate(mats, axis=0), n_lv


def _hgrn_body(c, n_lv, q_ref, f_ref, i_ref, g_ref, s0_ref, low_ref, nw_ref, l_ref,
               o_ref, sout_ref, st_scr, o_scr):
    step = pl.program_id(1)
    n_steps = pl.num_programs(1)

    @pl.when(step == 0)
    def _():
        for h in range(A_HEADS):
            st_scr[h] = s0_ref[0, h].T

    low = low_ref[...]
    f = low + (1.0 - low) * jax.nn.sigmoid(f_ref[...])
    lg = jnp.log(f)
    kk = 1.0 - f
    qz = q_ref[...]
    qq = qz * jax.nn.sigmoid(qz)
    vv = i_ref[...]

    hi = lg.astype(BF16)
    r1 = lg - hi.astype(F32)
    mid = r1.astype(BF16)
    lo = (r1 - mid.astype(F32)).astype(BF16)
    lmat = l_ref[...]
    z = (jnp.dot(lmat, hi, preferred_element_type=F32) + jnp.dot(lmat, mid, preferred_element_type=F32)
         + jnp.dot(lmat, lo, preferred_element_type=F32))

    row = lax.broadcasted_iota(I32, (c, 1), 0)
    rt = lax.broadcasted_iota(I32, (c, c), 0)
    rs = lax.broadcasted_iota(I32, (c, c), 1)
    qps, kps, same = [qq.astype(BF16)], [kk.astype(BF16)], [rt == rs]
    for lv in range(n_lv):
        e = jnp.exp(z[lv * c:(lv + 1) * c])
        upper = ((row >> lv) & 1) == 1
        qps.append(jnp.where(upper, qq * e, 0.0).astype(BF16))
        kps.append(jnp.where(upper, 0.0, kk * e).astype(BF16))
        same.append((rt >> (lv + 1)) == (rs >> (lv + 1)))
    b = z[n_lv * c:(n_lv + 1) * c]
    qb = (qq * jnp.exp(b)).astype(BF16)
    kc = (kk * jnp.exp(z[(n_lv + 1) * c:(n_lv + 2) * c])).astype(BF16)
    e_last = jnp.exp(b[c - 1:c, :])
    vb = vv.astype(BF16)

    for h in range(A_HEADS):
        sl = slice(h * A_DK, (h + 1) * A_DK)
        a = jnp.zeros((c, c), F32)
        for qp, kp, sm in zip(qps, kps, same):
            a = a + jnp.where(sm, _dot_nt(qp[:, sl], kp[:, sl]), 0.0)
        st = st_scr[h]
        o_h = _dot_nt(qb[:, sl], st.astype(BF16)) + jnp.dot(a.astype(BF16), vb[:, sl],
                                                             preferred_element_type=F32)
        o_scr[:, sl] = o_h
        upd = lax.dot_general(vb[:, sl], kc[:, sl], (((0,), (0,)), ((), ())), preferred_element_type=F32)
        st_scr[h] = st * e_last[:, sl] + upd

    o = o_scr[...]
    o = o * lax.rsqrt(jnp.mean(o * o, axis=-1, keepdims=True) + RMS_EPS) * nw_ref[...]
    gz = g_ref[...]
    o_ref[...] = o * (gz * jax.nn.sigmoid(gz))

    @pl.when(step == n_steps - 1)
    def _():
        for h in range(A_HEADS):
            sout_ref[0, h] = st_scr[h].T


def _hgrn(proj, row0, bsz, t, s0, lower, norm_w, c):
    d = D_MODEL
    assert row0 % c == 0 and t % c == 0
    lmat, n_lv = _hgrn_cumsum_matrix(c)
    lmat = jnp.asarray(lmat, BF16)
    blk = lambda col: pl.BlockSpec((c, d), lambda b, s, col=col: (row0 // c + b * (t // c) + s, col))
    st_spec = pl.BlockSpec((1, A_HEADS, A_DK, A_DK), lambda b, s: (b, 0, 0, 0))
    vec = pl.BlockSpec((1, d), lambda b, s: (0, 0))
    return pl.pallas_call(
        functools.partial(_hgrn_body, c, n_lv),
        grid=(bsz, t // c),
        in_specs=[blk(0), blk(1), blk(2), blk(3), st_spec, vec, vec,
                  pl.BlockSpec(lmat.shape, lambda b, s: (0, 0))],
        out_specs=[pl.BlockSpec((c, d), lambda b, s: (b * (t // c) + s, 0)), st_spec],
        out_shape=[jax.ShapeDtypeStruct((bsz * t, d), F32), jax.ShapeDtypeStruct(s0.shape, F32)],
        scratch_shapes=[pltpu.VMEM((A_HEADS, A_DK, A_DK), F32), pltpu.VMEM((c, d), F32)],
        compiler_params=_params(("arbitrary", "arbitrary")),
        name=f"hgrn_c{c}",
    )(proj, proj, proj, proj, s0, lower.reshape(1, d), norm_w.reshape(1, d), lmat)


def _bias_by_distance(rel_bias, n):
    dist = jnp.arange(n, dtype=I32)
    max_exact = N_BUCKETS // 2
    scaled = jnp.log(jnp.maximum(dist, max_exact).astype(F32) / max_exact) / math.log(MAX_DISTANCE / max_exact)
    large = jnp.minimum(max_exact + (scaled * (N_BUCKETS - max_exact)).astype(I32), N_BUCKETS - 1)
    bucket = jnp.where(dist < max_exact, dist, large)
    onehot = (bucket[:, None] == jnp.arange(N_BUCKETS, dtype=I32)[None, :]).astype(F32)
    return jnp.dot(onehot, rel_bias.astype(F32), precision=HIGHEST).T


def _toeplitz(w, n):
    h = w.shape[0]
    wext = jnp.concatenate([w, jnp.zeros((h, 1), w.dtype)], axis=1)
    m = jnp.tile(wext, (1, n))[:, :n * (2 * n - 1)].reshape(h, n, 2 * n - 1)
    return m[:, :, n - 1:]


def _top3_rows(g, rowb, n_rows):
    sel = jnp.zeros(g.shape, F32)
    for _ in range(MOBA_TOPK):
        m = jnp.max(g, axis=0, keepdims=True)
        ix = jnp.min(jnp.where(g == m, rowb, n_rows), axis=0, keepdims=True)
        pick = rowb == ix
        sel = jnp.where(pick & (m > -jnp.inf), 1.0, sel)
        g = jnp.where(pick, -jnp.inf, g)
    return sel


def _moba_p_body(n_blk, farb_ref, q_ref, k_ref, vt_ref, km_ref, bd_ref, bp_ref, o_ref,
                 row_scr, s_scr, sd_scr, mx_scr, l_scr, acc_scr):
    hp = pl.program_id(1)
    i = pl.program_id(2)
    blk = MOBA_BLOCK

    q = q_ref[...] * (B_HEAD_DIM ** -0.5)
    lane = lax.broadcasted_iota(I32, q.shape, 1)
    qh = [jnp.where(lane < B_HEAD_DIM, q, 0.0), jnp.where(lane >= B_HEAD_DIM, q, 0.0)]
    qhb = [(x * LOG2E).astype(BF16) for x in qh]
    km = km_ref[0]
    rowb = lax.broadcasted_iota(I32, (n_blk, blk), 0)
    for a in range(2):
        g = _dot_nt(km, qh[a], precision=HIGHEST)
        g = jnp.where(rowb < i, g, -jnp.inf)
        row_scr[a] = jnp.where(_top3_rows(g, rowb, n_blk) > 0.0, 0.0, NEG_INF)

    def k_block(j):
        return k_ref[pl.ds(pl.multiple_of(j * blk, blk), blk), :].astype(BF16)

    def col_max8(s):
        return jnp.max(s.reshape(blk // 8, 8, blk), axis=0)

    kd = k_block(i)
    for a in range(2):
        s = _dot_nt(kd, qhb[a]) + bd_ref[a]
        sd_scr[a] = s
        mx_scr[a] = col_max8(s)

    n_far = jnp.maximum(i - 1, 0)

    def far_logits(t, c):
        for u in range(2):
            j = 2 * t + u
            kb = k_block(j)
            live = j < n_far
            for a in range(2):
                row = jnp.where(live, row_scr[a, pl.ds(j, 1), :] + farb_ref[2 * hp + a], NEG_INF)
                s = _dot_nt(kb, qhb[a]) + row
                s_scr[a, j] = s
                mx_scr[a] = jnp.maximum(mx_scr[a], col_max8(s))
        return c
    lax.fori_loop(0, (n_far + 1) // 2, far_logits, 0)

    @pl.when(i >= 1)
    def _():
        kb = k_block(i - 1)
        for a in range(2):
            s = _dot_nt(kb, qhb[a]) + bp_ref[a] + row_scr[a, pl.ds(i - 1, 1), :]
            s_scr[a, i - 1] = s
            mx_scr[a] = jnp.maximum(mx_scr[a], col_max8(s))

    m = [jnp.max(mx_scr[a], axis=0, keepdims=True) for a in range(2)]
    vtd = vt_ref[0, i]
    for a in range(2):
        p = jnp.exp2(sd_scr[a] - m[a])
        l_scr[a] = jnp.sum(p, axis=0, keepdims=True)
        acc_scr[a] = jnp.dot(vtd, p.astype(BF16), preferred_element_type=F32)

    def past_values(t, c):
        lsum = [jnp.zeros((1, blk), F32), jnp.zeros((1, blk), F32)]
        pv = [jnp.zeros((LANES, blk), F32), jnp.zeros((LANES, blk), F32)]
        for u in range(2):
            j = 2 * t + u
            live = j < i
            jc = jnp.where(live, j, 0)
            vtb = vt_ref[0, jc]
            for a in range(2):
                p = jnp.where(live, jnp.exp2(s_scr[a, jc] - m[a]), 0.0)
                lsum[a] = lsum[a] + jnp.sum(p, axis=0, keepdims=True)
                pv[a] = pv[a] + jnp.dot(vtb, p.astype(BF16), preferred_element_type=F32)
        for a in range(2):
            l_scr[a] = l_scr[a] + lsum[a]
            acc_scr[a] = acc_scr[a] + pv[a]
        return c
    lax.fori_loop(0, (i + 1) // 2, past_values, 0)

    out0 = acc_scr[0] / l_scr[0]
    out1 = acc_scr[1] / l_scr[1]
    sub = lax.broadcasted_iota(I32, (LANES, blk), 0)
    o_ref[0] = jnp.where(sub < B_HEAD_DIM, out0, out1).T


def _moba_prompt(q, kv, vt, k_mean, bias_tbl, bsz, t):
    d = q.shape[1]
    blk = MOBA_BLOCK
    n_blk = t // blk
    n_hp = B_HEADS // 2
    bias2 = bias_tbl * LOG2E
    bias_diag = _toeplitz(jnp.concatenate([jnp.full((B_HEADS, blk - 1), NEG_INF, F32), bias2[:, :blk]], axis=1), blk)
    bias_prev = _toeplitz(bias2[:, 1:2 * blk], blk)
    far_bias = bias2[:, 2 * blk - 1]
    return pl.pallas_call(
        functools.partial(_moba_p_body, n_blk),
        grid=(bsz, n_hp, n_blk),
        in_specs=[pl.BlockSpec(memory_space=pltpu.SMEM),
                  pl.BlockSpec((blk, LANES), lambda b, h, i: (b * n_blk + i, h)),
                  pl.BlockSpec((t, LANES), lambda b, h, i: (b, h)),
                  pl.BlockSpec((1, n_blk, LANES, blk), lambda b, h, i: (b, 0, h, 0)),
                  pl.BlockSpec((1, n_blk, LANES), lambda b, h, i: (b, 0, h)),
                  pl.BlockSpec((2, blk, blk), lambda b, h, i: (h, 0, 0)),
                  pl.BlockSpec((2, blk, blk), lambda b, h, i: (h, 0, 0))],
        out_specs=pl.BlockSpec((1, blk, LANES), lambda b, h, i: (b, i, h)),
        out_shape=jax.ShapeDtypeStruct((bsz, t, d), F32),
        scratch_shapes=[pltpu.VMEM((2, n_blk, blk), F32),
                        pltpu.VMEM((2, n_blk, blk, blk), F32), pltpu.VMEM((2, blk, blk), F32),
                        pltpu.VMEM((2, 8, blk), F32), pltpu.VMEM((2, 1, blk), F32),
                        pltpu.VMEM((2, LANES, blk), F32)],
        compiler_params=_params(("arbitrary", "arbitrary", "arbitrary")),
        name="moba_prompt",
    )(far_bias, q, kv, vt, k_mean, bias_diag, bias_prev)


def _top3_lanes(g, lane, n_lanes):
    sel = jnp.zeros(g.shape, F32)
    for _ in range(MOBA_TOPK):
        m = jnp.max(g, axis=1, keepdims=True)
        ix = jnp.min(jnp.where(g == m, lane, n_lanes), axis=1, keepdims=True)
        pick = lane == ix
        sel = jnp.where(pick & (m > -jnp.inf), 1.0, sel)
        g = jnp.where(pick, -jnp.inf, g)
    return sel


def _moba_s_body(n_pages, t_new, pt_ref, qbt_ref, *refs):
    del pt_ref
    g = SAMPLE_PAGES_PER_STEP
    kt_refs, vt_refs = refs[:g], refs[g:2 * g]
    (kn_ref, vn_ref, blast_ref, bnew_ref, farb_ref, o_ref,
     s_scr, kmt_scr, sel_scr, kn_scr, vn_scr, acc_scr, l_scr, qb_scr) = refs[2 * g:]
    p = pl.program_id(1)
    page = LANES
    pages_per_blk = MOBA_BLOCK // page
    n_blk = n_pages // pages_per_blk
    n_steps = n_pages // g
    lane = lax.broadcasted_iota(I32, (LANES, LANES), 1)

    @pl.when(p == 0)
    def _():
        qb_scr[...] = qbt_ref[0].astype(BF16)
        kmt_scr[...] = jnp.zeros(kmt_scr.shape, F32)

    @pl.when(p < n_steps)
    def _():
        lane_k = lax.broadcasted_iota(I32, kmt_scr.shape, 1)
        for bi in range(g // pages_per_blk):
            ks = [kt_refs[bi * pages_per_blk + u][0] for u in range(pages_per_blk)]
            s = jnp.dot(qb_scr[...], jnp.concatenate(ks, axis=1).astype(BF16), preferred_element_type=F32)
            for u in range(pages_per_blk):
                s_scr[p * g + bi * pages_per_blk + u] = s[:, u * page:(u + 1) * page]
            tot = ks[0]
            for u in range(1, pages_per_blk):
                tot = tot + ks[u]
            blk_sum = jnp.sum(tot, axis=1, keepdims=True)
            b = p * (g // pages_per_blk) + bi
            kmt_scr[...] = kmt_scr[...] + jnp.where(lane_k == b, blk_sum, 0.0)

    @pl.when(p == n_steps - 1)
    def _():
        gate = jnp.dot(qbt_ref[0], kmt_scr[...], precision=HIGHEST, preferred_element_type=F32) * (1.0 / MOBA_BLOCK)
        gate = jnp.where(lane < n_blk, gate, -jnp.inf)
        sel_scr[...] = _top3_lanes(gate, lane, LANES)

        kn_scr[...] = jnp.zeros(kn_scr.shape, F32)
        kn_scr[0:t_new, :] = kn_ref[0]
        vn_scr[...] = jnp.zeros(vn_scr.shape, F32)
        vn_scr[0:t_new, :] = vn_ref[0]
        sn = _dot_nt(qb_scr[...], kn_scr[...].astype(BF16)) + bnew_ref[...]
        qi = lax.rem(lax.broadcasted_iota(I32, (LANES, LANES), 0), t_new)
        sn = jnp.where((lane < t_new) & (lane <= qi), sn, NEG_INF)
        farb = farb_ref[...]

        def pass_max(pg, macc):
            keep = jnp.sum(jnp.where(lane == pg // pages_per_blk, sel_scr[...], 0.0), axis=1, keepdims=True) > 0.0
            bias = jnp.where(pg == n_pages - 1, blast_ref[...], farb)
            s = jnp.where(keep, s_scr[pg] + bias, NEG_INF)
            s_scr[pg] = s
            return jnp.maximum(macc, s)
        m = jnp.max(lax.fori_loop(0, n_pages, pass_max, sn), axis=1, keepdims=True)

        def pass_exp(pg, lacc):
            e = jnp.exp(s_scr[pg] - m)
            s_scr[pg] = e
            return lacc + e
        en = jnp.exp(sn - m)
        l = jnp.sum(lax.fori_loop(0, n_pages, pass_exp, en), axis=1, keepdims=True)
        l_scr[...] = jnp.broadcast_to(l, l_scr.shape)
        acc_scr[...] = jnp.dot(en.astype(BF16), vn_scr[...].astype(BF16), preferred_element_type=F32)

    @pl.when(p >= n_steps)
    def _():
        pg0 = (p - n_steps) * g
        acc = acc_scr[...]
        for bi in range(g // pages_per_blk):
            first = bi * pages_per_blk
            pc = jnp.concatenate([s_scr[pg0 + first + u] for u in range(pages_per_blk)], axis=1).astype(BF16)
            vc = jnp.concatenate([vt_refs[first + u][0] for u in range(pages_per_blk)], axis=1).astype(BF16)
            acc = acc + _dot_nt(pc, vc)
        acc_scr[...] = acc

    @pl.when(p == 2 * n_steps - 1)
    def _():
        acc = acc_scr[...] / l_scr[:, 0:1]
        rh = lax.broadcasted_iota(I32, acc.shape, 0) // t_new
        ch = lax.broadcasted_iota(I32, acc.shape, 1) // B_HEAD_DIM
        acc = jnp.where(rh == ch, acc, 0.0)
        o_ref[0] = jnp.sum(acc.reshape(B_HEADS, t_new, D_MODEL), axis=0)


def _moba_sample(q, kv, cache_kt, cache_vt, page_table, bias_tbl):
    s, t_new, d = q.shape
    n_pages = page_table.shape[1]
    page = cache_kt.shape[2]
    g = SAMPLE_PAGES_PER_STEP
    pages_per_blk = MOBA_BLOCK // page
    assert B_HEADS * t_new == LANES and page == LANES and n_pages % g == 0 and g % pages_per_blk == 0
    assert n_pages // pages_per_blk <= LANES
    n_steps = n_pages // g
    scale = B_HEAD_DIM ** -0.5
    qh = (q * scale).reshape(s, t_new, B_HEADS, B_HEAD_DIM).transpose(0, 2, 1, 3)
    eye = jnp.eye(B_HEADS, dtype=F32)
    qbt = (qh[:, :, :, None, :] * eye[None, :, None, :, None]).reshape(s, LANES, d)
    bias_last = jnp.stack([jnp.flip(bias_tbl[:, 1 + i:1 + i + page], axis=1) for i in range(t_new)], axis=1)
    bias_last = bias_last.reshape(LANES, page)
    near = np.maximum(np.arange(t_new)[:, None] - np.arange(t_new)[None, :], 0)
    bias_new = jnp.stack([jnp.stack([bias_tbl[:, near[i, j]] for j in range(t_new)], axis=1)
                          for i in range(t_new)], axis=1).reshape(LANES, t_new)
    bias_new = jnp.pad(bias_new, ((0, 0), (0, LANES - t_new)))
    far_bias = jnp.broadcast_to(jnp.repeat(bias_tbl[:, 2 * MOBA_BLOCK - 1], t_new)[:, None], (LANES, LANES))
    k_new = kv[..., :d]
    v_new = kv[..., d:]
    kmaps = [lambda b, p, pt, u=u: (pt[b * n_pages + g * jnp.minimum(p, n_steps - 1) + u], 0, 0)
             for u in range(g)]
    vmaps = [lambda b, p, pt, u=u: (pt[b * n_pages + g * jnp.maximum(p - n_steps, 0) + u], 0, 0)
             for u in range(g)]
    seq = lambda b, p, pt: (b, 0, 0)
    fixed = lambda b, p, pt: (0, 0)
    grid_spec = pltpu.PrefetchScalarGridSpec(
        num_scalar_prefetch=1,
        grid=(s, 2 * n_steps),
        in_specs=[pl.BlockSpec((1, LANES, d), seq)]
                 + [pl.BlockSpec((1, d, page), m) for m in kmaps]
                 + [pl.BlockSpec((1, d, page), m) for m in vmaps]
                 + [pl.BlockSpec((1, t_new, d), seq), pl.BlockSpec((1, t_new, d), seq),
                    pl.BlockSpec((LANES, page), fixed), pl.BlockSpec((LANES, LANES), fixed),
                    pl.BlockSpec((LANES, LANES), fixed)],
        out_specs=pl.BlockSpec((1, t_new, d), seq),
        scratch_shapes=[pltpu.VMEM((n_pages, LANES, page), F32), pltpu.VMEM((d, LANES), F32),
                        pltpu.VMEM((LANES, LANES), F32), pltpu.VMEM((LANES, d), F32),
                        pltpu.VMEM((LANES, d), F32), pltpu.VMEM((LANES, d), F32),
                        pltpu.VMEM((LANES, LANES), F32), pltpu.VMEM((LANES, d), BF16)],
    )
    return pl.pallas_call(
        functools.partial(_moba_s_body, n_pages, t_new),
        grid_spec=grid_spec,
        out_shape=jax.ShapeDtypeStruct((s, t_new, d), F32),
        compiler_params=_params(("arbitrary", "arbitrary")),
        name="moba_sample",
    )(page_table.reshape(-1).astype(I32), qbt, *([cache_kt] * g), *([cache_vt] * g),
      k_new, v_new, bias_last, bias_new, far_bias)


def kernel(x_prompt, x_sample, cache_k, cache_v, state_hgrn, page_table, a_w_in, a_lb, a_norm, a_w_out,
           kv_w, b_w_q, b_w_o, rel_bias, router_w, router_b, w_up, b_up, w_down, b_down, ln_g, ln_b):
    bp, tp, d = x_prompt.shape
    bs, ts, _ = x_sample.shape
    n_p = bp * tp
    n_s = bs * ts
    x = jnp.concatenate([x_prompt.reshape(n_p, d), x_sample.reshape(n_s, d)], axis=0)

    p_lb = jax.nn.softmax(a_lb.astype(F32), axis=0)
    lower = jnp.cumsum(p_lb, axis=0) - p_lb[0]
    bias_tbl = _bias_by_distance(rel_bias, 2 * MOBA_BLOCK)
    zero_state = jnp.zeros((bp,) + state_hgrn.shape[2:], F32)
    hgrn_chunk = 64 if tp % 64 == 0 else tp

    states_p, states_s = [], []
    kv = None
    for l in range(DEPTH):
        if l < N_A_LAYERS:
            proj = _proj(x, a_w_in[l].astype(BF16))
            o_p, s_p = _hgrn(proj, 0, bp, tp, zero_state, lower[l], a_norm[l], hgrn_chunk)
            o_s, s_s = _hgrn(proj, n_p, bs, ts, state_hgrn[l], lower[l], a_norm[l], ts)
            states_p.append(s_p)
            states_s.append(s_s)
            mix = jnp.concatenate([o_p, o_s], axis=0)
            x = _mm_res_ln(x, mix, a_w_out[l].astype(BF16), ln_g[l, 0], ln_b[l, 0])
        else:
            j = l - N_A_LAYERS
            if kv is None:
                kv, vt_all, km_all = _kv_proj(x, kv_w.astype(BF16))
                kv_p = kv[:n_p].reshape(bp, tp, 2 * d)
                kv_s = kv[n_p:].reshape(bs, ts, 2 * d)
                n_kb = tp // MOBA_BLOCK
                vt_p = vt_all[:bp * n_kb].reshape(bp, n_kb, d, MOBA_BLOCK)
                km_p = km_all[:bp * n_kb].reshape(bp, n_kb, d)
                n_phys, page = cache_k.shape[:2]
                cache_kt = cache_k.transpose(0, 2, 3, 1).reshape(n_phys, d, page)
                cache_vt = cache_v.transpose(0, 2, 3, 1).reshape(n_phys, d, page)
            q = _proj(x, b_w_q[j].astype(BF16))
            o_p = _moba_prompt(q, kv, vt_p, km_p, bias_tbl, bp, tp)
            o_s = _moba_sample(q[n_p:].reshape(bs, ts, d), kv_s, cache_kt, cache_vt, page_table, bias_tbl)
            mix = jnp.concatenate([o_p.reshape(n_p, d), o_s.reshape(n_s, d)], axis=0)
            x = _mm_res_ln(x, mix, b_w_o[j].astype(BF16), ln_g[l, 0], ln_b[l, 0])
        bg = b_up[l, :, 0::2].reshape(N_EXPERTS, 1, d)
        bl = b_up[l, :, 1::2].reshape(N_EXPERTS, 1, d)
        x = _moe(x, l, router_w[l], router_b[l], w_up, bg, bl, w_down,
                 b_down[l].reshape(N_EXPERTS, 1, d), ln_g[l, 1], ln_b[l, 1])

    y_p = x[:n_p].reshape(bp, tp, d)
    y_s = x[n_p:].reshape(bs, ts, d)
    heads = (B_HEADS, B_HEAD_DIM)
    return (y_p, y_s,
            kv_p[..., :d].reshape(bp, tp, *heads), kv_p[..., d:].reshape(bp, tp, *heads),
            jnp.stack(states_p),
            kv_s[..., :d].reshape(bs, ts, *heads), kv_s[..., d:].reshape(bs, ts, *heads),
            jnp.stack(states_s))
```

```python
import functools
import math

import jax
import jax.numpy as jnp
import numpy as np
from jax import lax
from jax.experimental import pallas as pl
from jax.experimental.pallas import tpu as pltpu

F32 = jnp.float32
BF16 = jnp.bfloat16
I32 = jnp.int32
HIGHEST = lax.Precision.HIGHEST

D_MODEL = 1024
DEPTH = 4
N_A_LAYERS = 2
A_HEADS = 8
A_DK = 128
B_HEADS = 16
B_HEAD_DIM = 64
MOBA_BLOCK = 256
MOBA_TOPK = 3
N_BUCKETS = 32
MAX_DISTANCE = 128
N_EXPERTS = 32
TOP_K = 4
SWIGLU_LIMIT = 7.0
SWIGLU_ALPHA = 1.702
DN_ALPHA = (2 * DEPTH) ** 0.25
LN_EPS = 1e-5
RMS_EPS = 1e-6
NEG_INF = -1e30
LOG2E = math.log2(math.e)

LANES = 128
VMEM_LIMIT = 56 * 1024 * 1024
MOE_ROWS = 256
COMBINE_ROWS = 128
DISPATCH_ROWS = 128
MOBA_UNROLL = 8
SAMPLE_PAGES_PER_STEP = 8


def _params(sem, vmem=VMEM_LIMIT):
    return pltpu.CompilerParams(dimension_semantics=sem, vmem_limit_bytes=vmem)


def _row_tile(n, cap):
    best = None
    for t in range(8, min(n, cap) + 1, 8):
        if n % t == 0:
            best = t
    assert best is not None, n
    return best


def _dot_nt(a, b, precision=None):
    return lax.dot_general(a, b, (((1,), (1,)), ((), ())), precision=precision,
                           preferred_element_type=F32)


def _layer_norm(y, g, b):
    mu = jnp.mean(y, axis=-1, keepdims=True)
    yc = y - mu
    var = jnp.mean(yc * yc, axis=-1, keepdims=True)
    return yc * lax.rsqrt(var + LN_EPS) * g + b


def _proj_body(x_ref, w_ref, o_ref):
    o_ref[...] = jnp.dot(x_ref[...].astype(BF16), w_ref[...], preferred_element_type=F32)


def _proj(x, w_bf16):
    n, k = x.shape
    m = w_bf16.shape[1]
    tn = 1024
    tm = _row_tile(n, 640)
    return pl.pallas_call(
        _proj_body,
        grid=(m // tn, n // tm),
        in_specs=[pl.BlockSpec((tm, k), lambda j, i: (i, 0)),
                  pl.BlockSpec((k, tn), lambda j, i: (0, j))],
        out_specs=pl.BlockSpec((tm, tn), lambda j, i: (i, j)),
        out_shape=jax.ShapeDtypeStruct((n, m), F32),
        compiler_params=_params(("arbitrary", "arbitrary")),
        name="proj",
    )(x, w_bf16)


def _kv_proj_body(x_ref, w_ref, kv_ref, vt_ref, km_ref):
    y = jnp.dot(x_ref[...].astype(BF16), w_ref[...], preferred_element_type=F32)
    kv_ref[...] = y
    km_ref[0] = jnp.mean(y[:, :D_MODEL], axis=0, keepdims=True)
    vt_ref[0] = y[:, D_MODEL:].T.astype(BF16)


def _kv_proj(x, w_bf16):
    n, k = x.shape
    d = D_MODEL
    tm = MOBA_BLOCK
    assert n % tm == 0
    return pl.pallas_call(
        _kv_proj_body,
        grid=(n // tm,),
        in_specs=[pl.BlockSpec((tm, k), lambda i: (i, 0)), pl.BlockSpec((k, 2 * d), lambda i: (0, 0))],
        out_specs=[pl.BlockSpec((tm, 2 * d), lambda i: (i, 0)),
                   pl.BlockSpec((1, d, tm), lambda i: (i, 0, 0)),
                   pl.BlockSpec((1, 1, d), lambda i: (i, 0, 0))],
        out_shape=[jax.ShapeDtypeStruct((n, 2 * d), F32),
                   jax.ShapeDtypeStruct((n // tm, d, tm), BF16),
                   jax.ShapeDtypeStruct((n // tm, 1, d), F32)],
        compiler_params=_params(("arbitrary",)),
        name="kv_proj",
    )(x, w_bf16)


def _mm_res_ln_body(x_ref, a_ref, w_ref, g_ref, b_ref, o_ref):
    h = jnp.dot(a_ref[...].astype(BF16), w_ref[...], preferred_element_type=F32)
    o_ref[...] = _layer_norm(DN_ALPHA * x_ref[...] + h, g_ref[...], b_ref[...])


def _mm_res_ln(x, a, w_bf16, g, b):
    n, d = x.shape
    tm = _row_tile(n, 640)
    row = lambda i: (i, 0)
    fixed = lambda i: (0, 0)
    return pl.pallas_call(
        _mm_res_ln_body,
        grid=(n // tm,),
        in_specs=[pl.BlockSpec((tm, d), row), pl.BlockSpec((tm, d), row),
                  pl.BlockSpec((d, d), fixed), pl.BlockSpec((1, d), fixed), pl.BlockSpec((1, d), fixed)],
        out_specs=pl.BlockSpec((tm, d), row),
        out_shape=jax.ShapeDtypeStruct((n, d), F32),
        compiler_params=_params(("arbitrary",)),
        name="mm_res_ln",
    )(x, a, w_bf16, g.reshape(1, d), b.reshape(1, d))


def _router_body(x_ref, w_ref, b_ref, tri_ref, idx_ref, gate_ref, rank_ref, cnt_ref, carry_scr):
    @pl.when(pl.program_id(0) == 0)
    def _():
        carry_scr[...] = jnp.zeros(carry_scr.shape, F32)

    logits = jnp.dot(x_ref[...], w_ref[...], precision=HIGHEST, preferred_element_type=F32) + b_ref[...]
    tm = logits.shape[0]
    col = lax.broadcasted_iota(I32, logits.shape, 1)
    lane = lax.broadcasted_iota(I32, (tm, LANES), 1)
    vals, idxs = [], []
    cur = logits
    for _ in range(TOP_K):
        m = jnp.max(cur, axis=-1, keepdims=True)
        ix = jnp.min(jnp.where(cur == m, col, N_EXPERTS), axis=-1, keepdims=True)
        vals.append(m)
        idxs.append(ix)
        cur = jnp.where(col == ix, -jnp.inf, cur)
    es = [jnp.exp(v - vals[0]) for v in vals]
    tot = es[0] + es[1] + es[2] + es[3]
    member = jnp.zeros(logits.shape, F32)
    for k in range(TOP_K):
        member = member + jnp.where(col == idxs[k], 1.0, 0.0)
    before = carry_scr[...] + jnp.dot(tri_ref[...], member.astype(BF16), preferred_element_type=F32)
    carry_scr[...] = carry_scr[...] + jnp.sum(member, axis=0, keepdims=True)
    cnt_ref[...] = jnp.broadcast_to(carry_scr[...], cnt_ref.shape)

    idx_out = jnp.zeros((tm, LANES), I32)
    gate_out = jnp.zeros((tm, LANES), F32)
    rank_out = jnp.zeros((tm, LANES), I32)
    for k in range(TOP_K):
        rank_k = jnp.sum(jnp.where(col == idxs[k], before, 0.0), axis=-1, keepdims=True)
        idx_out = jnp.where(lane == k, idxs[k], idx_out)
        gate_out = jnp.where(lane == k, es[k] / tot, gate_out)
        rank_out = jnp.where(lane == k, rank_k.astype(I32), rank_out)
    idx_ref[...] = idx_out
    gate_ref[...] = gate_out
    rank_ref[...] = rank_out


def _router(x, w, b):
    n, d = x.shape
    tm = _row_tile(n, 640)
    row = lambda i: (i, 0)
    fixed = lambda i: (0, 0)
    tri = jnp.asarray(np.tril(np.ones((tm, tm), np.float32), -1), BF16)
    return pl.pallas_call(
        _router_body,
        grid=(n // tm,),
        in_specs=[pl.BlockSpec((tm, d), row), pl.BlockSpec((d, N_EXPERTS), fixed),
                  pl.BlockSpec((1, N_EXPERTS), fixed), pl.BlockSpec((tm, tm), fixed)],
        out_specs=[pl.BlockSpec((tm, LANES), row), pl.BlockSpec((tm, LANES), row),
                   pl.BlockSpec((tm, LANES), row), pl.BlockSpec((8, N_EXPERTS), fixed)],
        out_shape=[jax.ShapeDtypeStruct((n, LANES), I32), jax.ShapeDtypeStruct((n, LANES), F32),
                   jax.ShapeDtypeStruct((n, LANES), I32), jax.ShapeDtypeStruct((8, N_EXPERTS), F32)],
        scratch_shapes=[pltpu.VMEM((1, N_EXPERTS), F32)],
        compiler_params=_params(("arbitrary",)),
        name="router",
    )(x, w, b.reshape(1, N_EXPERTS), tri)


def _deinterleave_matrix():
    p = np.zeros((2 * LANES, 2 * LANES), np.float32)
    j = np.arange(LANES)
    p[2 * j, j] = 1.0
    p[2 * j + 1, LANES + j] = 1.0
    return p


def _dispatch_body(pos_ref, x_ref, xb_in, xb_hbm, buf, sem):
    del xb_in
    i = pl.program_id(0)
    nsteps = pl.num_programs(0)
    slot = lax.rem(i, 2)
    tm = DISPATCH_ROWS

    def wait(slot_):
        for _ in range(TOP_K):
            pltpu.make_async_copy(buf.at[slot_], xb_hbm.at[pl.ds(0, tm)], sem.at[slot_]).wait()

    @pl.when(i >= 2)
    def _():
        wait(slot)

    buf[slot] = x_ref[...]
    base = i * (tm * TOP_K)

    def body(g, c):
        for u in range(2):
            r = 2 * g + u
            for k in range(TOP_K):
                dst = pos_ref[base + r * TOP_K + k]
                pltpu.make_async_copy(buf.at[slot, pl.ds(r, 1)], xb_hbm.at[pl.ds(dst, 1)], sem.at[slot]).start()
        return c
    lax.fori_loop(0, tm // 2, body, 0)

    @pl.when(i == nsteps - 1)
    def _():
        wait(slot)

    @pl.when((i == nsteps - 1) & (i >= 1))
    def _():
        wait(1 - slot)


def _dispatch(x, pos_flat, n_rows):
    n, d = x.shape
    tm = DISPATCH_ROWS
    grid_spec = pltpu.PrefetchScalarGridSpec(
        num_scalar_prefetch=1,
        grid=(n // tm,),
        in_specs=[pl.BlockSpec((tm, d), lambda i, p: (i, 0)), pl.BlockSpec(memory_space=pl.ANY)],
        out_specs=pl.BlockSpec(memory_space=pl.ANY),
        scratch_shapes=[pltpu.VMEM((2, tm, d), F32), pltpu.SemaphoreType.DMA((2,))],
    )
    return pl.pallas_call(
        _dispatch_body,
        grid_spec=grid_spec,
        out_shape=jax.ShapeDtypeStruct((n_rows, d), F32),
        input_output_aliases={2: 0},
        compiler_params=_params(("arbitrary",)),
        name="dispatch",
    )(pos_flat, x, jnp.zeros((n_rows, d), F32))


def _expert_body(blk_e_ref, nblk_ref, x_ref, wu_ref, bg_ref, bl_ref, wd_ref, bd_ref, perm_ref, y_ref,
                 wg_scr, wl_scr, wd_scr):
    i = pl.program_id(0)
    nb = nblk_ref[0]
    changed = (i == 0) | (blk_e_ref[i] != blk_e_ref[jnp.maximum(i - 1, 0)])

    @pl.when((i < nb) & changed)
    def _():
        d = D_MODEL
        perm = perm_ref[...]
        for g in range(d // LANES):
            wblk = wu_ref[0, :, 2 * LANES * g:2 * LANES * (g + 1)].astype(BF16)
            t = jnp.dot(wblk, perm, preferred_element_type=F32)
            wg_scr[:, LANES * g:LANES * (g + 1)] = t[:, :LANES].astype(BF16)
            wl_scr[:, LANES * g:LANES * (g + 1)] = t[:, LANES:].astype(BF16)
        wd_scr[...] = wd_ref[0].astype(BF16)

    @pl.when(i < nb)
    def _():
        xb = x_ref[...].astype(BF16)
        hg = jnp.dot(xb, wg_scr[...], preferred_element_type=F32) + bg_ref[0]
        hl = jnp.dot(xb, wl_scr[...], preferred_element_type=F32) + bl_ref[0]
        glu = jnp.minimum(hg, SWIGLU_LIMIT)
        lin = jnp.clip(hl, -SWIGLU_LIMIT, SWIGLU_LIMIT)
        a = glu * jax.nn.sigmoid(SWIGLU_ALPHA * glu) * (lin + 1.0)
        y_ref[...] = jnp.dot(a.astype(BF16), wd_scr[...], preferred_element_type=F32) + bd_ref[0]

    @pl.when(i >= nb)
    def _():
        y_ref[...] = jnp.zeros(y_ref.shape, F32)


def _experts(xb, blk_e, n_used, layer, w_up, bg, bl, w_down, bd):
    d = xb.shape[1]
    n_blk = blk_e.shape[0]
    w_map = lambda i, be, nb: (be[i], 0, 0)
    lw_map = lambda i, be, nb: (layer, be[i], 0, 0)
    rows = lambda i, be, nb: (i, 0)
    perm = jnp.asarray(_deinterleave_matrix(), BF16)
    grid_spec = pltpu.PrefetchScalarGridSpec(
        num_scalar_prefetch=2,
        grid=(n_blk,),
        in_specs=[pl.BlockSpec((MOE_ROWS, d), rows),
                  pl.BlockSpec((None, 1, d, 2 * d), lw_map),
                  pl.BlockSpec((1, 1, d), w_map), pl.BlockSpec((1, 1, d), w_map),
                  pl.BlockSpec((None, 1, d, d), lw_map), pl.BlockSpec((1, 1, d), w_map),
                  pl.BlockSpec(perm.shape, lambda i, be, nb: (0, 0))],
        out_specs=pl.BlockSpec((MOE_ROWS, d), rows),
        scratch_shapes=[pltpu.VMEM((d, d), BF16), pltpu.VMEM((d, d), BF16), pltpu.VMEM((d, d), BF16)],
    )
    return pl.pallas_call(
        _expert_body,
        grid_spec=grid_spec,
        out_shape=jax.ShapeDtypeStruct((n_blk * MOE_ROWS, d), F32),
        compiler_params=_params(("arbitrary",)),
        name="experts",
    )(blk_e, n_used, xb, w_up, bg, bl, w_down, bd, perm)


def _combine_body(pos_ref, x_ref, gate_ref, y_hbm, g_ref, b_ref, o_ref, ybuf, sem):
    i = pl.program_id(0)
    nsteps = pl.num_programs(0)
    slot = lax.rem(i, 2)
    tm = COMBINE_ROWS

    def row_copy(src, slot_, k, r):
        return pltpu.make_async_copy(y_hbm.at[pl.ds(src, 1)], ybuf.at[slot_, k, pl.ds(r, 1)], sem.at[slot_])

    def issue(step, slot_):
        base = step * (tm * TOP_K)

        def body(g, c):
            for u in range(2):
                r = 2 * g + u
                for k in range(TOP_K):
                    row_copy(pos_ref[base + r * TOP_K + k], slot_, k, r).start()
            return c
        lax.fori_loop(0, tm // 2, body, 0)

    def wait(slot_):
        for k in range(TOP_K):
            pltpu.make_async_copy(y_hbm.at[pl.ds(0, tm)], ybuf.at[slot_, k], sem.at[slot_]).wait()

    @pl.when(i == 0)
    def _():
        issue(0, 0)

    @pl.when(i + 1 < nsteps)
    def _():
        issue(i + 1, 1 - slot)

    wait(slot)
    gates = gate_ref[...]
    m = gates[:, 0:1] * ybuf[slot, 0]
    for k in range(1, TOP_K):
        m = m + gates[:, k:k + 1] * ybuf[slot, k]
    o_ref[...] = _layer_norm(DN_ALPHA * x_ref[...] + m, g_ref[...], b_ref[...])


def _combine(x, gates, pos_flat, yb, g, b):
    n, d = x.shape
    tm = COMBINE_ROWS
    row = lambda i, p: (i, 0)
    fixed = lambda i, p: (0, 0)
    grid_spec = pltpu.PrefetchScalarGridSpec(
        num_scalar_prefetch=1,
        grid=(n // tm,),
        in_specs=[pl.BlockSpec((tm, d), row), pl.BlockSpec((tm, LANES), row),
                  pl.BlockSpec(memory_space=pl.ANY),
                  pl.BlockSpec((1, d), fixed), pl.BlockSpec((1, d), fixed)],
        out_specs=pl.BlockSpec((tm, d), row),
        scratch_shapes=[pltpu.VMEM((2, TOP_K, tm, d), F32), pltpu.SemaphoreType.DMA((2,))],
    )
    return pl.pallas_call(
        _combine_body,
        grid_spec=grid_spec,
        out_shape=jax.ShapeDtypeStruct((n, d), F32),
        compiler_params=_params(("arbitrary",)),
        name="combine",
    )(pos_flat, x, gates, yb, g.reshape(1, d), b.reshape(1, d))


def _moe(x, layer, router_w, router_b, w_up, bg, bl, w_down, bd, ln_g, ln_b):
    n, d = x.shape
    idx_pad, gates, rank_pad, cnt = _router(x, router_w, router_b)
    nk = n * TOP_K
    counts = cnt[0].astype(I32)
    padded = (counts + MOE_ROWS - 1) // MOE_ROWS * MOE_ROWS
    ends = jnp.cumsum(padded)
    pad_start = ends - padded
    idx4 = idx_pad[:, :TOP_K]
    start4 = jnp.sum(jnp.where(idx4[..., None] == jnp.arange(N_EXPERTS), pad_start, 0), axis=-1)
    pos_flat = (start4 + rank_pad[:, :TOP_K]).reshape(nk).astype(I32)
    n_blk = -(-nk // MOE_ROWS) + N_EXPERTS
    blk_first = jnp.arange(n_blk, dtype=I32) * MOE_ROWS
    blk_e = jnp.minimum(jnp.sum(ends[None, :] <= blk_first[:, None], axis=1), N_EXPERTS - 1).astype(I32)
    n_used = (ends[-1:] // MOE_ROWS).astype(I32)
    xb = _dispatch(x, pos_flat, n_blk * MOE_ROWS)
    yb = _experts(xb, blk_e, n_used, layer, w_up, bg, bl, w_down, bd)
    return _combine(x, gates, pos_flat, yb, ln_g, ln_b)


def _hgrn_cumsum_matrix(c):
    n_lv = int(math.log2(c))
    assert 1 << n_lv == c
    mats = []
    t = np.arange(c)
    for lv in range(n_lv):
        m = 1 << lv
        mat = np.zeros((c, c), np.float32)
        seg0 = (t // m) * m
        for r in range(c):
            if (r // m) % 2 == 1:
                mat[r, seg0[r]:r + 1] = 1.0
            else:
                mat[r, r + 1:seg0[r] + m] = 1.0
        mats.append(mat)
    mats.append(np.tril(np.ones((c, c), np.float32)))
    mats.append(np.triu(np.ones((c, c), np.float32), 1))
    return np.concatenate(mats, axis=0), n_lv


def _hgrn_body(c, n_lv, q_ref, f_ref, i_ref, g_ref, s0_ref, low_ref, nw_ref, l_ref,
               o_ref, sout_ref, st_scr, o_scr):
    step = pl.program_id(1)
    n_steps = pl.num_programs(1)

    @pl.when(step == 0)
    def _():
        for h in range(A_HEADS):
            st_scr[h] = s0_ref[0, h].T

    low = low_ref[...]
    f = low + (1.0 - low) * jax.nn.sigmoid(f_ref[...])
    lg = jnp.log(f)
    kk = 1.0 - f
    qz = q_ref[...]
    qq = qz * jax.nn.sigmoid(qz)
    vv = i_ref[...]

    hi = lg.astype(BF16)
    r1 = lg - hi.astype(F32)
    mid = r1.astype(BF16)
    lo = (r1 - mid.astype(F32)).astype(BF16)
    lmat = l_ref[...]
    z = (jnp.dot(lmat, hi, preferred_element_type=F32) + jnp.dot(lmat, mid, preferred_element_type=F32)
         + jnp.dot(lmat, lo, preferred_element_type=F32))

    row = lax.broadcasted_iota(I32, (c, 1), 0)
    rt = lax.broadcasted_iota(I32, (c, c), 0)
    rs = lax.broadcasted_iota(I32, (c, c), 1)
    qps, kps, same = [qq.astype(BF16)], [kk.astype(BF16)], [rt == rs]
    for lv in range(n_lv):
        e = jnp.exp(z[lv * c:(lv + 1) * c])
        upper = ((row >> lv) & 1) == 1
        qps.append(jnp.where(upper, qq * e, 0.0).astype(BF16))
        kps.append(jnp.where(upper, 0.0, kk * e).astype(BF16))
        same.append((rt >> (lv + 1)) == (rs >> (lv + 1)))
    b = z[n_lv * c:(n_lv + 1) * c]
    qb = (qq * jnp.exp(b)).astype(BF16)
    kc = (kk * jnp.exp(z[(n_lv + 1) * c:(n_lv + 2) * c])).astype(BF16)
    e_last = jnp.exp(b[c - 1:c, :])
    vb = vv.astype(BF16)

    for h in range(A_HEADS):
        sl = slice(h * A_DK, (h + 1) * A_DK)
        a = jnp.zeros((c, c), F32)
        for qp, kp, sm in zip(qps, kps, same):
            a = a + jnp.where(sm, _dot_nt(qp[:, sl], kp[:, sl]), 0.0)
        st = st_scr[h]
        o_h = _dot_nt(qb[:, sl], st.astype(BF16)) + jnp.dot(a.astype(BF16), vb[:, sl],
                                                             preferred_element_type=F32)
        o_scr[:, sl] = o_h
        upd = lax.dot_general(vb[:, sl], kc[:, sl], (((0,), (0,)), ((), ())), preferred_element_type=F32)
        st_scr[h] = st * e_last[:, sl] + upd

    o = o_scr[...]
    o = o * lax.rsqrt(jnp.mean(o * o, axis=-1, keepdims=True) + RMS_EPS) * nw_ref[...]
    gz = g_ref[...]
    o_ref[...] = o * (gz * jax.nn.sigmoid(gz))

    @pl.when(step == n_steps - 1)
    def _():
        for h in range(A_HEADS):
            sout_ref[0, h] = st_scr[h].T


def _hgrn(proj, row0, bsz, t, s0, lower, norm_w, c):
    d = D_MODEL
    assert row0 % c == 0 and t % c == 0
    lmat, n_lv = _hgrn_cumsum_matrix(c)
    lmat = jnp.asarray(lmat, BF16)
    blk = lambda col: pl.BlockSpec((c, d), lambda b, s, col=col: (row0 // c + b * (t // c) + s, col))
    st_spec = pl.BlockSpec((1, A_HEADS, A_DK, A_DK), lambda b, s: (b, 0, 0, 0))
    vec = pl.BlockSpec((1, d), lambda b, s: (0, 0))
    return pl.pallas_call(
        functools.partial(_hgrn_body, c, n_lv),
        grid=(bsz, t // c),
        in_specs=[blk(0), blk(1), blk(2), blk(3), st_spec, vec, vec,
                  pl.BlockSpec(lmat.shape, lambda b, s: (0, 0))],
        out_specs=[pl.BlockSpec((c, d), lambda b, s: (b * (t // c) + s, 0)), st_spec],
        out_shape=[jax.ShapeDtypeStruct((bsz * t, d), F32), jax.ShapeDtypeStruct(s0.shape, F32)],
        scratch_shapes=[pltpu.VMEM((A_HEADS, A_DK, A_DK), F32), pltpu.VMEM((c, d), F32)],
        compiler_params=_params(("arbitrary", "arbitrary")),
        name=f"hgrn_c{c}",
    )(proj, proj, proj, proj, s0, lower.reshape(1, d), norm_w.reshape(1, d), lmat)


def _bias_by_distance(rel_bias, n):
    dist = jnp.arange(n, dtype=I32)
    max_exact = N_BUCKETS // 2
    scaled = jnp.log(jnp.maximum(dist, max_exact).astype(F32) / max_exact) / math.log(MAX_DISTANCE / max_exact)
    large = jnp.minimum(max_exact + (scaled * (N_BUCKETS - max_exact)).astype(I32), N_BUCKETS - 1)
    bucket = jnp.where(dist < max_exact, dist, large)
    onehot = (bucket[:, None] == jnp.arange(N_BUCKETS, dtype=I32)[None, :]).astype(F32)
    return jnp.dot(onehot, rel_bias.astype(F32), precision=HIGHEST).T


def _toeplitz(w, n):
    h = w.shape[0]
    wext = jnp.concatenate([w, jnp.zeros((h, 1), w.dtype)], axis=1)
    m = jnp.tile(wext, (1, n))[:, :n * (2 * n - 1)].reshape(h, n, 2 * n - 1)
    return m[:, :, n - 1:]


def _top3_rows(g, rowb, n_rows):
    sel = jnp.zeros(g.shape, F32)
    for _ in range(MOBA_TOPK):
        m = jnp.max(g, axis=0, keepdims=True)
        ix = jnp.min(jnp.where(g == m, rowb, n_rows), axis=0, keepdims=True)
        pick = rowb == ix
        sel = jnp.where(pick & (m > -jnp.inf), 1.0, sel)
        g = jnp.where(pick, -jnp.inf, g)
    return sel


def _moba_p_body(n_blk, farb_ref, q_ref, k_ref, vt_ref, km_ref, bd_ref, bp_ref, o_ref,
                 row_scr, s_scr, sd_scr, mx_scr, l_scr, acc_scr):
    hp = pl.program_id(1)
    i = pl.program_id(2)
    blk = MOBA_BLOCK

    q = q_ref[...] * (B_HEAD_DIM ** -0.5)
    lane = lax.broadcasted_iota(I32, q.shape, 1)
    qh = [jnp.where(lane < B_HEAD_DIM, q, 0.0), jnp.where(lane >= B_HEAD_DIM, q, 0.0)]
    qhb = [(x * LOG2E).astype(BF16) for x in qh]
    km = km_ref[0]
    rowb = lax.broadcasted_iota(I32, (n_blk, blk), 0)
    for a in range(2):
        g = _dot_nt(km, qh[a], precision=HIGHEST)
        g = jnp.where(rowb < i, g, -jnp.inf)
        row_scr[a] = jnp.where(_top3_rows(g, rowb, n_blk) > 0.0, 0.0, NEG_INF)

    def k_block(j):
        return k_ref[pl.ds(pl.multiple_of(j * blk, blk), blk), :].astype(BF16)

    def col_max8(s):
        return jnp.max(s.reshape(blk // 8, 8, blk), axis=0)

    kd = k_block(i)
    for a in range(2):
        s = _dot_nt(kd, qhb[a]) + bd_ref[a]
        sd_scr[a] = s
        mx_scr[a] = col_max8(s)

    n_far = jnp.maximum(i - 1, 0)

    def far_logits(t, c):
        for u in range(MOBA_UNROLL):
            j = jnp.minimum(MOBA_UNROLL * t + u, n_blk - 1)
            kb = k_block(j)
            live = MOBA_UNROLL * t + u < n_far
            for a in range(2):
                row = jnp.where(live, row_scr[a, pl.ds(j, 1), :] + farb_ref[2 * hp + a], NEG_INF)
                s = _dot_nt(kb, qhb[a]) + row
                s_scr[a, j] = s
                mx_scr[a] = jnp.maximum(mx_scr[a], col_max8(s))
        return c
    lax.fori_loop(0, (n_far + MOBA_UNROLL - 1) // MOBA_UNROLL, far_logits, 0)

    @pl.when(i >= 1)
    def _():
        kb = k_block(i - 1)
        for a in range(2):
            s = _dot_nt(kb, qhb[a]) + bp_ref[a] + row_scr[a, pl.ds(i - 1, 1), :]
            s_scr[a, i - 1] = s
            mx_scr[a] = jnp.maximum(mx_scr[a], col_max8(s))

    m = [jnp.max(mx_scr[a], axis=0, keepdims=True) for a in range(2)]
    vtd = vt_ref[0, i]
    for a in range(2):
        p = jnp.exp2(sd_scr[a] - m[a])
        l_scr[a] = jnp.sum(p, axis=0, keepdims=True)
        acc_scr[a] = jnp.dot(vtd, p.astype(BF16), preferred_element_type=F32)

    def past_values(t, c):
        lsum = [jnp.zeros((1, blk), F32), jnp.zeros((1, blk), F32)]
        pv = [jnp.zeros((LANES, blk), F32), jnp.zeros((LANES, blk), F32)]
        for u in range(MOBA_UNROLL):
            j = MOBA_UNROLL * t + u
            live = j < i
            jc = jnp.where(live, j, 0)
            vtb = vt_ref[0, jc]
            for a in range(2):
                p = jnp.where(live, jnp.exp2(s_scr[a, jc] - m[a]), 0.0)
                lsum[a] = lsum[a] + jnp.sum(p, axis=0, keepdims=True)
                pv[a] = pv[a] + jnp.dot(vtb, p.astype(BF16), preferred_element_type=F32)
        for a in range(2):
            l_scr[a] = l_scr[a] + lsum[a]
            acc_scr[a] = acc_scr[a] + pv[a]
        return c
    lax.fori_loop(0, (i + MOBA_UNROLL - 1) // MOBA_UNROLL, past_values, 0)

    out0 = acc_scr[0] / l_scr[0]
    out1 = acc_scr[1] / l_scr[1]
    sub = lax.broadcasted_iota(I32, (LANES, blk), 0)
    o_ref[0] = jnp.where(sub < B_HEAD_DIM, out0, out1).T


def _moba_prompt(q, kv, vt, k_mean, bias_tbl, bsz, t):
    d = q.shape[1]
    blk = MOBA_BLOCK
    n_blk = t // blk
    n_hp = B_HEADS // 2
    bias2 = bias_tbl * LOG2E
    bias_diag = _toeplitz(jnp.concatenate([jnp.full((B_HEADS, blk - 1), NEG_INF, F32), bias2[:, :blk]], axis=1), blk)
    bias_prev = _toeplitz(bias2[:, 1:2 * blk], blk)
    far_bias = bias2[:, 2 * blk - 1]
    return pl.pallas_call(
        functools.partial(_moba_p_body, n_blk),
        grid=(bsz, n_hp, n_blk),
        in_specs=[pl.BlockSpec(memory_space=pltpu.SMEM),
                  pl.BlockSpec((blk, LANES), lambda b, h, i: (b * n_blk + i, h)),
                  pl.BlockSpec((t, LANES), lambda b, h, i: (b, h)),
                  pl.BlockSpec((1, n_blk, LANES, blk), lambda b, h, i: (b, 0, h, 0)),
                  pl.BlockSpec((1, n_blk, LANES), lambda b, h, i: (b, 0, h)),
                  pl.BlockSpec((2, blk, blk), lambda b, h, i: (h, 0, 0)),
                  pl.BlockSpec((2, blk, blk), lambda b, h, i: (h, 0, 0))],
        out_specs=pl.BlockSpec((1, blk, LANES), lambda b, h, i: (b, i, h)),
        out_shape=jax.ShapeDtypeStruct((bsz, t, d), F32),
        scratch_shapes=[pltpu.VMEM((2, n_blk, blk), F32),
                        pltpu.VMEM((2, n_blk, blk, blk), F32), pltpu.VMEM((2, blk, blk), F32),
                        pltpu.VMEM((2, 8, blk), F32), pltpu.VMEM((2, 1, blk), F32),
                        pltpu.VMEM((2, LANES, blk), F32)],
        compiler_params=_params(("arbitrary", "arbitrary", "arbitrary")),
        name="moba_prompt",
    )(far_bias, q, kv, vt, k_mean, bias_diag, bias_prev)


def _top3_lanes(g, lane, n_lanes):
    sel = jnp.zeros(g.shape, F32)
    for _ in range(MOBA_TOPK):
        m = jnp.max(g, axis=1, keepdims=True)
        ix = jnp.min(jnp.where(g == m, lane, n_lanes), axis=1, keepdims=True)
        pick = lane == ix
        sel = jnp.where(pick & (m > -jnp.inf), 1.0, sel)
        g = jnp.where(pick, -jnp.inf, g)
    return sel


def _moba_s_body(n_pages, t_new, pt_ref, qbt_ref, *refs):
    del pt_ref
    g = SAMPLE_PAGES_PER_STEP
    kt_refs, vt_refs = refs[:g], refs[g:2 * g]
    (kn_ref, vn_ref, blast_ref, bnew_ref, farb_ref, o_ref,
     s_scr, kmt_scr, sel_scr, kn_scr, vn_scr, acc_scr, l_scr, qb_scr) = refs[2 * g:]
    p = pl.program_id(1)
    page = LANES
    pages_per_blk = MOBA_BLOCK // page
    n_blk = n_pages // pages_per_blk
    n_steps = n_pages // g
    lane = lax.broadcasted_iota(I32, (LANES, LANES), 1)

    @pl.when(p == 0)
    def _():
        qb_scr[...] = qbt_ref[0].astype(BF16)
        kmt_scr[...] = jnp.zeros(kmt_scr.shape, F32)

    @pl.when(p < n_steps)
    def _():
        lane_k = lax.broadcasted_iota(I32, kmt_scr.shape, 1)
        for bi in range(g // pages_per_blk):
            ks = [kt_refs[bi * pages_per_blk + u][0] for u in range(pages_per_blk)]
            s = jnp.dot(qb_scr[...], jnp.concatenate(ks, axis=1).astype(BF16), preferred_element_type=F32)
            for u in range(pages_per_blk):
                s_scr[p * g + bi * pages_per_blk + u] = s[:, u * page:(u + 1) * page]
            tot = ks[0]
            for u in range(1, pages_per_blk):
                tot = tot + ks[u]
            blk_sum = jnp.sum(tot, axis=1, keepdims=True)
            b = p * (g // pages_per_blk) + bi
            kmt_scr[...] = kmt_scr[...] + jnp.where(lane_k == b, blk_sum, 0.0)

    @pl.when(p == n_steps - 1)
    def _():
        gate = jnp.dot(qbt_ref[0], kmt_scr[...], precision=HIGHEST, preferred_element_type=F32) * (1.0 / MOBA_BLOCK)
        gate = jnp.where(lane < n_blk, gate, -jnp.inf)
        sel_scr[...] = _top3_lanes(gate, lane, LANES)

        kn_scr[...] = jnp.zeros(kn_scr.shape, F32)
        kn_scr[0:t_new, :] = kn_ref[0]
        vn_scr[...] = jnp.zeros(vn_scr.shape, F32)
        vn_scr[0:t_new, :] = vn_ref[0]
        sn = _dot_nt(qb_scr[...], kn_scr[...].astype(BF16)) + bnew_ref[...]
        qi = lax.rem(lax.broadcasted_iota(I32, (LANES, LANES), 0), t_new)
        sn = jnp.where((lane < t_new) & (lane <= qi), sn, NEG_INF)
        farb = farb_ref[...]

        def pass_max(pg, macc):
            keep = jnp.sum(jnp.where(lane == pg // pages_per_blk, sel_scr[...], 0.0), axis=1, keepdims=True) > 0.0
            bias = jnp.where(pg == n_pages - 1, blast_ref[...], farb)
            s = jnp.where(keep, s_scr[pg] + bias, NEG_INF)
            s_scr[pg] = s
            return jnp.maximum(macc, s)
        m = jnp.max(lax.fori_loop(0, n_pages, pass_max, sn), axis=1, keepdims=True)

        def pass_exp(pg, lacc):
            e = jnp.exp(s_scr[pg] - m)
            s_scr[pg] = e
            return lacc + e
        en = jnp.exp(sn - m)
        l = jnp.sum(lax.fori_loop(0, n_pages, pass_exp, en), axis=1, keepdims=True)
        l_scr[...] = jnp.broadcast_to(l, l_scr.shape)
        acc_scr[...] = jnp.dot(en.astype(BF16), vn_scr[...].astype(BF16), preferred_element_type=F32)

    @pl.when(p >= n_steps)
    def _():
        pg0 = (p - n_steps) * g
        acc = acc_scr[...]
        for bi in range(g // pages_per_blk):
            first = bi * pages_per_blk
            pc = jnp.concatenate([s_scr[pg0 + first + u] for u in range(pages_per_blk)], axis=1).astype(BF16)
            vc = jnp.concatenate([vt_refs[first + u][0] for u in range(pages_per_blk)], axis=1).astype(BF16)
            acc = acc + _dot_nt(pc, vc)
        acc_scr[...] = acc

    @pl.when(p == 2 * n_steps - 1)
    def _():
        acc = acc_scr[...] / l_scr[:, 0:1]
        rh = lax.broadcasted_iota(I32, acc.shape, 0) // t_new
        ch = lax.broadcasted_iota(I32, acc.shape, 1) // B_HEAD_DIM
        acc = jnp.where(rh == ch, acc, 0.0)
        o_ref[0] = jnp.sum(acc.reshape(B_HEADS, t_new, D_MODEL), axis=0)


def _moba_sample(q, kv, cache_kt, cache_vt, page_table, bias_tbl):
    s, t_new, d = q.shape
    n_pages = page_table.shape[1]
    page = cache_kt.shape[2]
    g = SAMPLE_PAGES_PER_STEP
    pages_per_blk = MOBA_BLOCK // page
    assert B_HEADS * t_new == LANES and page == LANES and n_pages % g == 0 and g % pages_per_blk == 0
    assert n_pages // pages_per_blk <= LANES
    n_steps = n_pages // g
    scale = B_HEAD_DIM ** -0.5
    qh = (q * scale).reshape(s, t_new, B_HEADS, B_HEAD_DIM).transpose(0, 2, 1, 3)
    eye = jnp.eye(B_HEADS, dtype=F32)
    qbt = (qh[:, :, :, None, :] * eye[None, :, None, :, None]).reshape(s, LANES, d)
    bias_last = jnp.stack([jnp.flip(bias_tbl[:, 1 + i:1 + i + page], axis=1) for i in range(t_new)], axis=1)
    bias_last = bias_last.reshape(LANES, page)
    near = np.maximum(np.arange(t_new)[:, None] - np.arange(t_new)[None, :], 0)
    bias_new = jnp.stack([jnp.stack([bias_tbl[:, near[i, j]] for j in range(t_new)], axis=1)
                          for i in range(t_new)], axis=1).reshape(LANES, t_new)
    bias_new = jnp.pad(bias_new, ((0, 0), (0, LANES - t_new)))
    far_bias = jnp.broadcast_to(jnp.repeat(bias_tbl[:, 2 * MOBA_BLOCK - 1], t_new)[:, None], (LANES, LANES))
    k_new = kv[..., :d]
    v_new = kv[..., d:]
    kmaps = [lambda b, p, pt, u=u: (pt[b * n_pages + g * jnp.minimum(p, n_steps - 1) + u], 0, 0)
             for u in range(g)]
    vmaps = [lambda b, p, pt, u=u: (pt[b * n_pages + g * jnp.maximum(p - n_steps, 0) + u], 0, 0)
             for u in range(g)]
    seq = lambda b, p, pt: (b, 0, 0)
    fixed = lambda b, p, pt: (0, 0)
    grid_spec = pltpu.PrefetchScalarGridSpec(
        num_scalar_prefetch=1,
        grid=(s, 2 * n_steps),
        in_specs=[pl.BlockSpec((1, LANES, d), seq)]
                 + [pl.BlockSpec((1, d, page), m) for m in kmaps]
                 + [pl.BlockSpec((1, d, page), m) for m in vmaps]
                 + [pl.BlockSpec((1, t_new, d), seq), pl.BlockSpec((1, t_new, d), seq),
                    pl.BlockSpec((LANES, page), fixed), pl.BlockSpec((LANES, LANES), fixed),
                    pl.BlockSpec((LANES, LANES), fixed)],
        out_specs=pl.BlockSpec((1, t_new, d), seq),
        scratch_shapes=[pltpu.VMEM((n_pages, LANES, page), F32), pltpu.VMEM((d, LANES), F32),
                        pltpu.VMEM((LANES, LANES), F32), pltpu.VMEM((LANES, d), F32),
                        pltpu.VMEM((LANES, d), F32), pltpu.VMEM((LANES, d), F32),
                        pltpu.VMEM((LANES, LANES), F32), pltpu.VMEM((LANES, d), BF16)],
    )
    return pl.pallas_call(
        functools.partial(_moba_s_body, n_pages, t_new),
        grid_spec=grid_spec,
        out_shape=jax.ShapeDtypeStruct((s, t_new, d), F32),
        compiler_params=_params(("arbitrary", "arbitrary")),
        name="moba_sample",
    )(page_table.reshape(-1).astype(I32), qbt, *([cache_kt] * g), *([cache_vt] * g),
      k_new, v_new, bias_last, bias_new, far_bias)


def kernel(x_prompt, x_sample, cache_k, cache_v, state_hgrn, page_table, a_w_in, a_lb, a_norm, a_w_out,
           kv_w, b_w_q, b_w_o, rel_bias, router_w, router_b, w_up, b_up, w_down, b_down, ln_g, ln_b):
    bp, tp, d = x_prompt.shape
    bs, ts, _ = x_sample.shape
    n_p = bp * tp
    n_s = bs * ts
    x = jnp.concatenate([x_prompt.reshape(n_p, d), x_sample.reshape(n_s, d)], axis=0)

    p_lb = jax.nn.softmax(a_lb.astype(F32), axis=0)
    lower = jnp.cumsum(p_lb, axis=0) - p_lb[0]
    bias_tbl = _bias_by_distance(rel_bias, 2 * MOBA_BLOCK)
    zero_state = jnp.zeros((bp,) + state_hgrn.shape[2:], F32)
    hgrn_chunk = 128 if tp % 128 == 0 else tp

    states_p, states_s = [], []
    kv = None
    for l in range(DEPTH):
        if l < N_A_LAYERS:
            proj = _proj(x, a_w_in[l].astype(BF16))
            o_p, s_p = _hgrn(proj, 0, bp, tp, zero_state, lower[l], a_norm[l], hgrn_chunk)
            o_s, s_s = _hgrn(proj, n_p, bs, ts, state_hgrn[l], lower[l], a_norm[l], ts)
            states_p.append(s_p)
            states_s.append(s_s)
            mix = jnp.concatenate([o_p, o_s], axis=0)
            x = _mm_res_ln(x, mix, a_w_out[l].astype(BF16), ln_g[l, 0], ln_b[l, 0])
        else:
            j = l - N_A_LAYERS
            if kv is None:
                kv, vt_all, km_all = _kv_proj(x, kv_w.astype(BF16))
                kv_p = kv[:n_p].reshape(bp, tp, 2 * d)
                kv_s = kv[n_p:].reshape(bs, ts, 2 * d)
                n_kb = tp // MOBA_BLOCK
                vt_p = vt_all[:bp * n_kb].reshape(bp, n_kb, d, MOBA_BLOCK)
                km_p = km_all[:bp * n_kb].reshape(bp, n_kb, d)
                n_phys, page = cache_k.shape[:2]
                cache_kt = cache_k.transpose(0, 2, 3, 1).reshape(n_phys, d, page)
                cache_vt = cache_v.transpose(0, 2, 3, 1).reshape(n_phys, d, page)
            q = _proj(x, b_w_q[j].astype(BF16))
            o_p = _moba_prompt(q, kv, vt_p, km_p, bias_tbl, bp, tp)
            o_s = _moba_sample(q[n_p:].reshape(bs, ts, d), kv_s, cache_kt, cache_vt, page_table, bias_tbl)
            mix = jnp.concatenate([o_p.reshape(n_p, d), o_s.reshape(n_s, d)], axis=0)
            x = _mm_res_ln(x, mix, b_w_o[j].astype(BF16), ln_g[l, 0], ln_b[l, 0])
        bg = b_up[l, :, 0::2].reshape(N_EXPERTS, 1, d)
        bl = b_up[l, :, 1::2].reshape(N_EXPERTS, 1, d)
        x = _moe(x, l, router_w[l], router_b[l], w_up, bg, bl, w_down,
                 b_down[l].reshape(N_EXPERTS, 1, d), ln_g[l, 1], ln_b[l, 1])

    y_p = x[:n_p].reshape(bp, tp, d)
    y_s = x[n_p:].reshape(bs, ts, d)
    heads = (B_HEADS, B_HEAD_DIM)
    return (y_p, y_s,
            kv_p[..., :d].reshape(bp, tp, *heads), kv_p[..., d:].reshape(bp, tp, *heads),
            jnp.stack(states_p),
            kv_s[..., :d].reshape(bs, ts, *heads), kv_s[..., d:].reshape(bs, ts, *heads),
            jnp.stack(states_s))
```

```python
import functools
import math

import jax
import jax.numpy as jnp
import numpy as np
from jax import lax
from jax.experimental import pallas as pl
from jax.experimental.pallas import tpu as pltpu

F32 = jnp.float32
BF16 = jnp.bfloat16
I32 = jnp.int32
HIGHEST = lax.Precision.HIGHEST

D_MODEL = 1024
DEPTH = 4
N_A_LAYERS = 2
A_HEADS = 8
A_DK = 128
B_HEADS = 16
B_HEAD_DIM = 64
MOBA_BLOCK = 256
MOBA_TOPK = 3
N_BUCKETS = 32
MAX_DISTANCE = 128
N_EXPERTS = 32
TOP_K = 4
SWIGLU_LIMIT = 7.0
SWIGLU_ALPHA = 1.702
DN_ALPHA = (2 * DEPTH) ** 0.25
LN_EPS = 1e-5
RMS_EPS = 1e-6
NEG_INF = -1e30
LOG2E = math.log2(math.e)

LANES = 128
VMEM_LIMIT = 56 * 1024 * 1024
MOE_ROWS = 256
COMBINE_ROWS = 128
DISPATCH_ROWS = 128
MOBA_UNROLL = 8
SAMPLE_PAGES_PER_STEP = 8


def _params(sem, vmem=VMEM_LIMIT):
    return pltpu.CompilerParams(dimension_semantics=sem, vmem_limit_bytes=vmem)


def _row_tile(n, cap):
    best = None
    for t in range(8, min(n, cap) + 1, 8):
        if n % t == 0:
            best = t
    assert best is not None, n
    return best


def _dot_nt(a, b, precision=None):
    return lax.dot_general(a, b, (((1,), (1,)), ((), ())), precision=precision,
                           preferred_element_type=F32)


def _layer_norm(y, g, b):
    mu = jnp.mean(y, axis=-1, keepdims=True)
    yc = y - mu
    var = jnp.mean(yc * yc, axis=-1, keepdims=True)
    return yc * lax.rsqrt(var + LN_EPS) * g + b


def _proj_body(x_ref, w_ref, o_ref):
    o_ref[...] = jnp.dot(x_ref[...].astype(BF16), w_ref[...], preferred_element_type=F32)


def _proj(x, w_bf16):
    n, k = x.shape
    m = w_bf16.shape[1]
    tn = 1024
    tm = _row_tile(n, 640)
    return pl.pallas_call(
        _proj_body,
        grid=(m // tn, n // tm),
        in_specs=[pl.BlockSpec((tm, k), lambda j, i: (i, 0)),
                  pl.BlockSpec((k, tn), lambda j, i: (0, j))],
        out_specs=pl.BlockSpec((tm, tn), lambda j, i: (i, j)),
        out_shape=jax.ShapeDtypeStruct((n, m), F32),
        compiler_params=_params(("arbitrary", "arbitrary")),
        name="proj",
    )(x, w_bf16)


def _kv_proj_body(x_ref, w_ref, kv_ref, vt_ref, km_ref):
    y = jnp.dot(x_ref[...].astype(BF16), w_ref[...], preferred_element_type=F32)
    kv_ref[...] = y
    km_ref[0] = jnp.mean(y[:, :D_MODEL], axis=0, keepdims=True)
    vt_ref[0] = y[:, D_MODEL:].T.astype(BF16)


def _kv_proj(x, w_bf16):
    n, k = x.shape
    d = D_MODEL
    tm = MOBA_BLOCK
    assert n % tm == 0
    return pl.pallas_call(
        _kv_proj_body,
        grid=(n // tm,),
        in_specs=[pl.BlockSpec((tm, k), lambda i: (i, 0)), pl.BlockSpec((k, 2 * d), lambda i: (0, 0))],
        out_specs=[pl.BlockSpec((tm, 2 * d), lambda i: (i, 0)),
                   pl.BlockSpec((1, d, tm), lambda i: (i, 0, 0)),
                   pl.BlockSpec((1, 1, d), lambda i: (i, 0, 0))],
        out_shape=[jax.ShapeDtypeStruct((n, 2 * d), F32),
                   jax.ShapeDtypeStruct((n // tm, d, tm), BF16),
                   jax.ShapeDtypeStruct((n // tm, 1, d), F32)],
        compiler_params=_params(("arbitrary",)),
        name="kv_proj",
    )(x, w_bf16)


def _mm_res_ln_body(x_ref, a_ref, w_ref, g_ref, b_ref, o_ref):
    h = jnp.dot(a_ref[...].astype(BF16), w_ref[...], preferred_element_type=F32)
    o_ref[...] = _layer_norm(DN_ALPHA * x_ref[...] + h, g_ref[...], b_ref[...])


def _mm_res_ln(x, a, w_bf16, g, b):
    n, d = x.shape
    tm = _row_tile(n, 640)
    row = lambda i: (i, 0)
    fixed = lambda i: (0, 0)
    return pl.pallas_call(
        _mm_res_ln_body,
        grid=(n // tm,),
        in_specs=[pl.BlockSpec((tm, d), row), pl.BlockSpec((tm, d), row),
                  pl.BlockSpec((d, d), fixed), pl.BlockSpec((1, d), fixed), pl.BlockSpec((1, d), fixed)],
        out_specs=pl.BlockSpec((tm, d), row),
        out_shape=jax.ShapeDtypeStruct((n, d), F32),
        compiler_params=_params(("arbitrary",)),
        name="mm_res_ln",
    )(x, a, w_bf16, g.reshape(1, d), b.reshape(1, d))


def _router_body(x_ref, w_ref, b_ref, tri_ref, idx_ref, gate_ref, rank_ref, cnt_ref, carry_scr):
    @pl.when(pl.program_id(0) == 0)
    def _():
        carry_scr[...] = jnp.zeros(carry_scr.shape, F32)

    logits = jnp.dot(x_ref[...], w_ref[...], precision=HIGHEST, preferred_element_type=F32) + b_ref[...]
    tm = logits.shape[0]
    col = lax.broadcasted_iota(I32, logits.shape, 1)
    lane = lax.broadcasted_iota(I32, (tm, LANES), 1)
    vals, idxs = [], []
    cur = logits
    for _ in range(TOP_K):
        m = jnp.max(cur, axis=-1, keepdims=True)
        ix = jnp.min(jnp.where(cur == m, col, N_EXPERTS), axis=-1, keepdims=True)
        vals.append(m)
        idxs.append(ix)
        cur = jnp.where(col == ix, -jnp.inf, cur)
    es = [jnp.exp(v - vals[0]) for v in vals]
    tot = es[0] + es[1] + es[2] + es[3]
    member = jnp.zeros(logits.shape, F32)
    for k in range(TOP_K):
        member = member + jnp.where(col == idxs[k], 1.0, 0.0)
    before = carry_scr[...] + jnp.dot(tri_ref[...], member.astype(BF16), preferred_element_type=F32)
    carry_scr[...] = carry_scr[...] + jnp.sum(member, axis=0, keepdims=True)
    cnt_ref[...] = jnp.broadcast_to(carry_scr[...], cnt_ref.shape)

    idx_out = jnp.zeros((tm, LANES), I32)
    gate_out = jnp.zeros((tm, LANES), F32)
    rank_out = jnp.zeros((tm, LANES), I32)
    for k in range(TOP_K):
        rank_k = jnp.sum(jnp.where(col == idxs[k], before, 0.0), axis=-1, keepdims=True)
        idx_out = jnp.where(lane == k, idxs[k], idx_out)
        gate_out = jnp.where(lane == k, es[k] / tot, gate_out)
        rank_out = jnp.where(lane == k, rank_k.astype(I32), rank_out)
    idx_ref[...] = idx_out
    gate_ref[...] = gate_out
    rank_ref[...] = rank_out


def _router(x, w, b):
    n, d = x.shape
    tm = _row_tile(n, 640)
    row = lambda i: (i, 0)
    fixed = lambda i: (0, 0)
    tri = jnp.asarray(np.tril(np.ones((tm, tm), np.float32), -1), BF16)
    return pl.pallas_call(
        _router_body,
        grid=(n // tm,),
        in_specs=[pl.BlockSpec((tm, d), row), pl.BlockSpec((d, N_EXPERTS), fixed),
                  pl.BlockSpec((1, N_EXPERTS), fixed), pl.BlockSpec((tm, tm), fixed)],
        out_specs=[pl.BlockSpec((tm, LANES), row), pl.BlockSpec((tm, LANES), row),
                   pl.BlockSpec((tm, LANES), row), pl.BlockSpec((8, N_EXPERTS), fixed)],
        out_shape=[jax.ShapeDtypeStruct((n, LANES), I32), jax.ShapeDtypeStruct((n, LANES), F32),
                   jax.ShapeDtypeStruct((n, LANES), I32), jax.ShapeDtypeStruct((8, N_EXPERTS), F32)],
        scratch_shapes=[pltpu.VMEM((1, N_EXPERTS), F32)],
        compiler_params=_params(("arbitrary",)),
        name="router",
    )(x, w, b.reshape(1, N_EXPERTS), tri)


def _deinterleave_matrix():
    p = np.zeros((2 * LANES, 2 * LANES), np.float32)
    j = np.arange(LANES)
    p[2 * j, j] = 1.0
    p[2 * j + 1, LANES + j] = 1.0
    return p


def _dispatch_body(pos_ref, x_ref, xb_in, xb_hbm, buf, sem):
    del xb_in
    i = pl.program_id(0)
    nsteps = pl.num_programs(0)
    slot = lax.rem(i, 2)
    tm = DISPATCH_ROWS

    def wait(slot_):
        for _ in range(TOP_K):
            pltpu.make_async_copy(buf.at[slot_], xb_hbm.at[pl.ds(0, tm)], sem.at[slot_]).wait()

    @pl.when(i >= 2)
    def _():
        wait(slot)

    buf[slot] = x_ref[...]
    base = i * (tm * TOP_K)

    def body(g, c):
        for u in range(2):
            r = 2 * g + u
            for k in range(TOP_K):
                dst = pos_ref[base + r * TOP_K + k]
                pltpu.make_async_copy(buf.at[slot, pl.ds(r, 1)], xb_hbm.at[pl.ds(dst, 1)], sem.at[slot]).start()
        return c
    lax.fori_loop(0, tm // 2, body, 0)

    @pl.when(i == nsteps - 1)
    def _():
        wait(slot)

    @pl.when((i == nsteps - 1) & (i >= 1))
    def _():
        wait(1 - slot)


def _dispatch(x, pos_flat, n_rows):
    n, d = x.shape
    tm = DISPATCH_ROWS
    grid_spec = pltpu.PrefetchScalarGridSpec(
        num_scalar_prefetch=1,
        grid=(n // tm,),
        in_specs=[pl.BlockSpec((tm, d), lambda i, p: (i, 0)), pl.BlockSpec(memory_space=pl.ANY)],
        out_specs=pl.BlockSpec(memory_space=pl.ANY),
        scratch_shapes=[pltpu.VMEM((2, tm, d), F32), pltpu.SemaphoreType.DMA((2,))],
    )
    return pl.pallas_call(
        _dispatch_body,
        grid_spec=grid_spec,
        out_shape=jax.ShapeDtypeStruct((n_rows, d), F32),
        input_output_aliases={2: 0},
        compiler_params=_params(("arbitrary",)),
        name="dispatch",
    )(pos_flat, x, jnp.zeros((n_rows, d), F32))


def _expert_body(blk_e_ref, nblk_ref, x_ref, wu_ref, bg_ref, bl_ref, wd_ref, bd_ref, perm_ref, y_ref,
                 wg_scr, wl_scr, wd_scr):
    i = pl.program_id(0)
    nb = nblk_ref[0]
    changed = (i == 0) | (blk_e_ref[i] != blk_e_ref[jnp.maximum(i - 1, 0)])

    @pl.when((i < nb) & changed)
    def _():
        d = D_MODEL
        perm = perm_ref[...]
        for g in range(d // LANES):
            wblk = wu_ref[0, :, 2 * LANES * g:2 * LANES * (g + 1)].astype(BF16)
            t = jnp.dot(wblk, perm, preferred_element_type=F32)
            wg_scr[:, LANES * g:LANES * (g + 1)] = t[:, :LANES].astype(BF16)
            wl_scr[:, LANES * g:LANES * (g + 1)] = t[:, LANES:].astype(BF16)
        wd_scr[...] = wd_ref[0].astype(BF16)

    @pl.when(i < nb)
    def _():
        xb = x_ref[...].astype(BF16)
        hg = jnp.dot(xb, wg_scr[...], preferred_element_type=F32) + bg_ref[0]
        hl = jnp.dot(xb, wl_scr[...], preferred_element_type=F32) + bl_ref[0]
        glu = jnp.minimum(hg, SWIGLU_LIMIT)
        lin = jnp.clip(hl, -SWIGLU_LIMIT, SWIGLU_LIMIT)
        a = glu * jax.nn.sigmoid(SWIGLU_ALPHA * glu) * (lin + 1.0)
        y_ref[...] = jnp.dot(a.astype(BF16), wd_scr[...], preferred_element_type=F32) + bd_ref[0]

    @pl.when(i >= nb)
    def _():
        y_ref[...] = jnp.zeros(y_ref.shape, F32)


def _experts(xb, blk_e, n_used, layer, w_up, bg, bl, w_down, bd):
    d = xb.shape[1]
    n_blk = blk_e.shape[0]
    w_map = lambda i, be, nb: (be[i], 0, 0)
    lw_map = lambda i, be, nb: (layer, be[i], 0, 0)
    rows = lambda i, be, nb: (i, 0)
    perm = jnp.asarray(_deinterleave_matrix(), BF16)
    grid_spec = pltpu.PrefetchScalarGridSpec(
        num_scalar_prefetch=2,
        grid=(n_blk,),
        in_specs=[pl.BlockSpec((MOE_ROWS, d), rows),
                  pl.BlockSpec((None, 1, d, 2 * d), lw_map),
                  pl.BlockSpec((1, 1, d), w_map), pl.BlockSpec((1, 1, d), w_map),
                  pl.BlockSpec((None, 1, d, d), lw_map), pl.BlockSpec((1, 1, d), w_map),
                  pl.BlockSpec(perm.shape, lambda i, be, nb: (0, 0))],
        out_specs=pl.BlockSpec((MOE_ROWS, d), rows),
        scratch_shapes=[pltpu.VMEM((d, d), BF16), pltpu.VMEM((d, d), BF16), pltpu.VMEM((d, d), BF16)],
    )
    return pl.pallas_call(
        _expert_body,
        grid_spec=grid_spec,
        out_shape=jax.ShapeDtypeStruct((n_blk * MOE_ROWS, d), F32),
        compiler_params=_params(("arbitrary",)),
        name="experts",
    )(blk_e, n_used, xb, w_up, bg, bl, w_down, bd, perm)


def _combine_body(pos_ref, x_ref, gate_ref, y_hbm, g_ref, b_ref, o_ref, ybuf, sem):
    i = pl.program_id(0)
    nsteps = pl.num_programs(0)
    slot = lax.rem(i, 2)
    tm = COMBINE_ROWS

    def row_copy(src, slot_, k, r):
        return pltpu.make_async_copy(y_hbm.at[pl.ds(src, 1)], ybuf.at[slot_, k, pl.ds(r, 1)], sem.at[slot_])

    def issue(step, slot_):
        base = step * (tm * TOP_K)

        def body(g, c):
            for u in range(2):
                r = 2 * g + u
                for k in range(TOP_K):
                    row_copy(pos_ref[base + r * TOP_K + k], slot_, k, r).start()
            return c
        lax.fori_loop(0, tm // 2, body, 0)

    def wait(slot_):
        for k in range(TOP_K):
            pltpu.make_async_copy(y_hbm.at[pl.ds(0, tm)], ybuf.at[slot_, k], sem.at[slot_]).wait()

    @pl.when(i == 0)
    def _():
        issue(0, 0)

    @pl.when(i + 1 < nsteps)
    def _():
        issue(i + 1, 1 - slot)

    wait(slot)
    gates = gate_ref[...]
    m = gates[:, 0:1] * ybuf[slot, 0]
    for k in range(1, TOP_K):
        m = m + gates[:, k:k + 1] * ybuf[slot, k]
    o_ref[...] = _layer_norm(DN_ALPHA * x_ref[...] + m, g_ref[...], b_ref[...])


def _combine(x, gates, pos_flat, yb, g, b):
    n, d = x.shape
    tm = COMBINE_ROWS
    row = lambda i, p: (i, 0)
    fixed = lambda i, p: (0, 0)
    grid_spec = pltpu.PrefetchScalarGridSpec(
        num_scalar_prefetch=1,
        grid=(n // tm,),
        in_specs=[pl.BlockSpec((tm, d), row), pl.BlockSpec((tm, LANES), row),
                  pl.BlockSpec(memory_space=pl.ANY),
                  pl.BlockSpec((1, d), fixed), pl.BlockSpec((1, d), fixed)],
        out_specs=pl.BlockSpec((tm, d), row),
        scratch_shapes=[pltpu.VMEM((2, TOP_K, tm, d), F32), pltpu.SemaphoreType.DMA((2,))],
    )
    return pl.pallas_call(
        _combine_body,
        grid_spec=grid_spec,
        out_shape=jax.ShapeDtypeStruct((n, d), F32),
        compiler_params=_params(("arbitrary",)),
        name="combine",
    )(pos_flat, x, gates, yb, g.reshape(1, d), b.reshape(1, d))


def _moe(x, layer, router_w, router_b, w_up, bg, bl, w_down, bd, ln_g, ln_b):
    n, d = x.shape
    idx_pad, gates, rank_pad, cnt = _router(x, router_w, router_b)
    nk = n * TOP_K
    counts = cnt[0].astype(I32)
    padded = (counts + MOE_ROWS - 1) // MOE_ROWS * MOE_ROWS
    ends = jnp.cumsum(padded)
    pad_start = ends - padded
    idx4 = idx_pad[:, :TOP_K]
    start4 = jnp.sum(jnp.where(idx4[..., None] == jnp.arange(N_EXPERTS), pad_start, 0), axis=-1)
    pos_flat = (start4 + rank_pad[:, :TOP_K]).reshape(nk).astype(I32)
    n_blk = -(-nk // MOE_ROWS) + N_EXPERTS
    blk_first = jnp.arange(n_blk, dtype=I32) * MOE_ROWS
    blk_e = jnp.minimum(jnp.sum(ends[None, :] <= blk_first[:, None], axis=1), N_EXPERTS - 1).astype(I32)
    n_used = (ends[-1:] // MOE_ROWS).astype(I32)
    xb = _dispatch(x, pos_flat, n_blk * MOE_ROWS)
    yb = _experts(xb, blk_e, n_used, layer, w_up, bg, bl, w_down, bd)
    return _combine(x, gates, pos_flat, yb, ln_g, ln_b)


def _hgrn_cumsum_matrix(c):
    n_lv = int(math.log2(c))
    assert 1 << n_lv == c
    mats = []
    t = np.arange(c)
    for lv in range(n_lv):
        m = 1 << lv
        mat = np.zeros((c, c), np.float32)
        seg0 = (t // m) * m
        for r in range(c):
            if (r // m) % 2 == 1:
                mat[r, seg0[r]:r + 1] = 1.0
            else:
                mat[r, r + 1:seg0[r] + m] = 1.0
        mats.append(mat)
    mats.append(np.tril(np.ones((c, c), np.float32)))
    mats.append(np.triu(np.ones((c, c), np.float32), 1))
    return np.concatenate(mats, axis=0), n_lv


def _hgrn_body(c, n_lv, q_ref, f_ref, i_ref, g_ref, s0_ref, low_ref, nw_ref, l_ref,
               o_ref, sout_ref, st_scr, o_scr):
    step = pl.program_id(1)
    n_steps = pl.num_programs(1)

    @pl.when(step == 0)
    def _():
        for h in range(A_HEADS):
            st_scr[h] = s0_ref[0, h].T

    low = low_ref[...]
    f = low + (1.0 - low) * jax.nn.sigmoid(f_ref[...])
    lg = jnp.log(f)
    kk = 1.0 - f
    qz = q_ref[...]
    qq = qz * jax.nn.sigmoid(qz)
    vv = i_ref[...]

    hi = lg.astype(BF16)
    r1 = lg - hi.astype(F32)
    mid = r1.astype(BF16)
    lo = (r1 - mid.astype(F32)).astype(BF16)
    lmat = l_ref[...]
    z = (jnp.dot(lmat, hi, preferred_element_type=F32) + jnp.dot(lmat, mid, preferred_element_type=F32)
         + jnp.dot(lmat, lo, preferred_element_type=F32))

    row = lax.broadcasted_iota(I32, (c, 1), 0)
    rt = lax.broadcasted_iota(I32, (c, c), 0)
    rs = lax.broadcasted_iota(I32, (c, c), 1)
    qps, kps, same = [qq.astype(BF16)], [kk.astype(BF16)], [rt == rs]
    for lv in range(n_lv):
        e = jnp.exp(z[lv * c:(lv + 1) * c])
        upper = ((row >> lv) & 1) == 1
        qps.append(jnp.where(upper, qq * e, 0.0).astype(BF16))
        kps.append(jnp.where(upper, 0.0, kk * e).astype(BF16))
        same.append((rt >> (lv + 1)) == (rs >> (lv + 1)))
    b = z[n_lv * c:(n_lv + 1) * c]
    qb = (qq * jnp.exp(b)).astype(BF16)
    kc = (kk * jnp.exp(z[(n_lv + 1) * c:(n_lv + 2) * c])).astype(BF16)
    e_last = jnp.exp(b[c - 1:c, :])
    vb = vv.astype(BF16)

    for h in range(A_HEADS):
        sl = slice(h * A_DK, (h + 1) * A_DK)
        a = jnp.zeros((c, c), F32)
        for qp, kp, sm in zip(qps, kps, same):
            a = a + jnp.where(sm, _dot_nt(qp[:, sl], kp[:, sl]), 0.0)
        st = st_scr[h]
        o_h = _dot_nt(qb[:, sl], st.astype(BF16)) + jnp.dot(a.astype(BF16), vb[:, sl],
                                                             preferred_element_type=F32)
        o_scr[:, sl] = o_h
        upd = lax.dot_general(vb[:, sl], kc[:, sl], (((0,), (0,)), ((), ())), preferred_element_type=F32)
        st_scr[h] = st * e_last[:, sl] + upd

    o = o_scr[...]
    o = o * lax.rsqrt(jnp.mean(o * o, axis=-1, keepdims=True) + RMS_EPS) * nw_ref[...]
    gz = g_ref[...]
    o_ref[...] = o * (gz * jax.nn.sigmoid(gz))

    @pl.when(step == n_steps - 1)
    def _():
        for h in range(A_HEADS):
            sout_ref[0, h] = st_scr[h].T


def _hgrn(proj, row0, bsz, t, s0, lower, norm_w, c):
    d = D_MODEL
    assert row0 % c == 0 and t % c == 0
    lmat, n_lv = _hgrn_cumsum_matrix(c)
    lmat = jnp.asarray(lmat, BF16)
    blk = lambda col: pl.BlockSpec((c, d), lambda b, s, col=col: (row0 // c + b * (t // c) + s, col))
    st_spec = pl.BlockSpec((1, A_HEADS, A_DK, A_DK), lambda b, s: (b, 0, 0, 0))
    vec = pl.BlockSpec((1, d), lambda b, s: (0, 0))
    return pl.pallas_call(
        functools.partial(_hgrn_body, c, n_lv),
        grid=(bsz, t // c),
        in_specs=[blk(0), blk(1), blk(2), blk(3), st_spec, vec, vec,
                  pl.BlockSpec(lmat.shape, lambda b, s: (0, 0))],
        out_specs=[pl.BlockSpec((c, d), lambda b, s: (b * (t // c) + s, 0)), st_spec],
        out_shape=[jax.ShapeDtypeStruct((bsz * t, d), F32), jax.ShapeDtypeStruct(s0.shape, F32)],
        scratch_shapes=[pltpu.VMEM((A_HEADS, A_DK, A_DK), F32), pltpu.VMEM((c, d), F32)],
        compiler_params=_params(("arbitrary", "arbitrary")),
        name=f"hgrn_c{c}",
    )(proj, proj, proj, proj, s0, lower.reshape(1, d), norm_w.reshape(1, d), lmat)


def _bias_by_distance(rel_bias, n):
    dist = jnp.arange(n, dtype=I32)
    max_exact = N_BUCKETS // 2
    scaled = jnp.log(jnp.maximum(dist, max_exact).astype(F32) / max_exact) / math.log(MAX_DISTANCE / max_exact)
    large = jnp.minimum(max_exact + (scaled * (N_BUCKETS - max_exact)).astype(I32), N_BUCKETS - 1)
    bucket = jnp.where(dist < max_exact, dist, large)
    onehot = (bucket[:, None] == jnp.arange(N_BUCKETS, dtype=I32)[None, :]).astype(F32)
    return jnp.dot(onehot, rel_bias.astype(F32), precision=HIGHEST).T


def _toeplitz(w, n):
    h = w.shape[0]
    wext = jnp.concatenate([w, jnp.zeros((h, 1), w.dtype)], axis=1)
    m = jnp.tile(wext, (1, n))[:, :n * (2 * n - 1)].reshape(h, n, 2 * n - 1)
    return m[:, :, n - 1:]


def _top3_rows(g, rowb, n_rows):
    sel = jnp.zeros(g.shape, F32)
    for _ in range(MOBA_TOPK):
        m = jnp.max(g, axis=0, keepdims=True)
        ix = jnp.min(jnp.where(g == m, rowb, n_rows), axis=0, keepdims=True)
        pick = rowb == ix
        sel = jnp.where(pick & (m > -jnp.inf), 1.0, sel)
        g = jnp.where(pick, -jnp.inf, g)
    return sel


def _moba_p_body(n_blk, farb_ref, q_ref, k_ref, vt_ref, km_ref, bd_ref, bp_ref, o_ref,
                 row_scr, s_scr, sd_scr, mx_scr, l_scr, acc_scr):
    hp = pl.program_id(1)
    i = pl.program_id(2)
    blk = MOBA_BLOCK

    q = q_ref[...] * (B_HEAD_DIM ** -0.5)
    lane = lax.broadcasted_iota(I32, q.shape, 1)
    qh = [jnp.where(lane < B_HEAD_DIM, q, 0.0), jnp.where(lane >= B_HEAD_DIM, q, 0.0)]
    qhb = [(x * LOG2E).astype(BF16) for x in qh]
    km = km_ref[0]
    rowb = lax.broadcasted_iota(I32, (n_blk, blk), 0)
    for a in range(2):
        g = _dot_nt(km, qh[a], precision=HIGHEST)
        g = jnp.where(rowb < i, g, -jnp.inf)
        row_scr[a] = jnp.where(_top3_rows(g, rowb, n_blk) > 0.0, 0.0, NEG_INF)

    @pl.when(i == 0)
    def _():
        for a in range(2):
            s_scr[a, n_blk] = jnp.full((blk, blk), NEG_INF, F32)

    def k_block(j):
        return k_ref[pl.ds(pl.multiple_of(j * blk, blk), blk), :].astype(BF16)

    def col_max8(s):
        return jnp.max(s.reshape(blk // 8, 8, blk), axis=0)

    kd = k_block(i)
    for a in range(2):
        s = _dot_nt(kd, qhb[a]) + bd_ref[a]
        sd_scr[a] = s
        mx_scr[a] = col_max8(s)

    n_far = jnp.maximum(i - 1, 0)

    def far_logits(t, c):
        for u in range(MOBA_UNROLL):
            j = jnp.minimum(MOBA_UNROLL * t + u, n_blk - 1)
            kb = k_block(j)
            live = MOBA_UNROLL * t + u < n_far
            for a in range(2):
                row = jnp.where(live, row_scr[a, pl.ds(j, 1), :] + farb_ref[2 * hp + a], NEG_INF)
                s = _dot_nt(kb, qhb[a]) + row
                s_scr[a, j] = s
                mx_scr[a] = jnp.maximum(mx_scr[a], col_max8(s))
        return c
    lax.fori_loop(0, (n_far + MOBA_UNROLL - 1) // MOBA_UNROLL, far_logits, 0)

    @pl.when(i >= 1)
    def _():
        kb = k_block(i - 1)
        for a in range(2):
            s = _dot_nt(kb, qhb[a]) + bp_ref[a] + row_scr[a, pl.ds(i - 1, 1), :]
            s_scr[a, i - 1] = s
            mx_scr[a] = jnp.maximum(mx_scr[a], col_max8(s))

    m = [jnp.max(mx_scr[a], axis=0, keepdims=True) for a in range(2)]
    vtd = vt_ref[0, i]
    for a in range(2):
        p = jnp.exp2(sd_scr[a] - m[a])
        l_scr[a] = jnp.sum(p, axis=0, keepdims=True)
        acc_scr[a] = jnp.dot(vtd, p.astype(BF16), preferred_element_type=F32)

    def past_values(t, c):
        lsum = [jnp.zeros((1, blk), F32), jnp.zeros((1, blk), F32)]
        pv = [jnp.zeros((LANES, blk), F32), jnp.zeros((LANES, blk), F32)]
        for u in range(MOBA_UNROLL):
            j = MOBA_UNROLL * t + u
            live = j < i
            vtb = vt_ref[0, jnp.where(live, j, 0)]
            js = jnp.where(live, j, n_blk)
            for a in range(2):
                p = jnp.exp2(s_scr[a, js] - m[a])
                lsum[a] = lsum[a] + jnp.sum(p, axis=0, keepdims=True)
                pv[a] = pv[a] + jnp.dot(vtb, p.astype(BF16), preferred_element_type=F32)
        for a in range(2):
            l_scr[a] = l_scr[a] + lsum[a]
            acc_scr[a] = acc_scr[a] + pv[a]
        return c
    lax.fori_loop(0, (i + MOBA_UNROLL - 1) // MOBA_UNROLL, past_values, 0)

    out0 = acc_scr[0] / l_scr[0]
    out1 = acc_scr[1] / l_scr[1]
    sub = lax.broadcasted_iota(I32, (LANES, blk), 0)
    o_ref[0] = jnp.where(sub < B_HEAD_DIM, out0, out1).T


def _moba_prompt(q, kv, vt, k_mean, bias_tbl, bsz, t):
    d = q.shape[1]
    blk = MOBA_BLOCK
    n_blk = t // blk
    n_hp = B_HEADS // 2
    bias2 = bias_tbl * LOG2E
    bias_diag = _toeplitz(jnp.concatenate([jnp.full((B_HEADS, blk - 1), NEG_INF, F32), bias2[:, :blk]], axis=1), blk)
    bias_prev = _toeplitz(bias2[:, 1:2 * blk], blk)
    far_bias = bias2[:, 2 * blk - 1]
    return pl.pallas_call(
        functools.partial(_moba_p_body, n_blk),
        grid=(bsz, n_hp, n_blk),
        in_specs=[pl.BlockSpec(memory_space=pltpu.SMEM),
                  pl.BlockSpec((blk, LANES), lambda b, h, i: (b * n_blk + i, h)),
                  pl.BlockSpec((t, LANES), lambda b, h, i: (b, h)),
                  pl.BlockSpec((1, n_blk, LANES, blk), lambda b, h, i: (b, 0, h, 0)),
                  pl.BlockSpec((1, n_blk, LANES), lambda b, h, i: (b, 0, h)),
                  pl.BlockSpec((2, blk, blk), lambda b, h, i: (h, 0, 0)),
                  pl.BlockSpec((2, blk, blk), lambda b, h, i: (h, 0, 0))],
        out_specs=pl.BlockSpec((1, blk, LANES), lambda b, h, i: (b, i, h)),
        out_shape=jax.ShapeDtypeStruct((bsz, t, d), F32),
        scratch_shapes=[pltpu.VMEM((2, n_blk, blk), F32),
                        pltpu.VMEM((2, n_blk + 1, blk, blk), F32), pltpu.VMEM((2, blk, blk), F32),
                        pltpu.VMEM((2, 8, blk), F32), pltpu.VMEM((2, 1, blk), F32),
                        pltpu.VMEM((2, LANES, blk), F32)],
        compiler_params=_params(("arbitrary", "arbitrary", "arbitrary")),
        name="moba_prompt",
    )(far_bias, q, kv, vt, k_mean, bias_diag, bias_prev)


def _top3_lanes(g, lane, n_lanes):
    sel = jnp.zeros(g.shape, F32)
    for _ in range(MOBA_TOPK):
        m = jnp.max(g, axis=1, keepdims=True)
        ix = jnp.min(jnp.where(g == m, lane, n_lanes), axis=1, keepdims=True)
        pick = lane == ix
        sel = jnp.where(pick & (m > -jnp.inf), 1.0, sel)
        g = jnp.where(pick, -jnp.inf, g)
    return sel


def _moba_s_body(n_pages, t_new, pt_ref, qbt_ref, *refs):
    del pt_ref
    g = SAMPLE_PAGES_PER_STEP
    kt_refs, vt_refs = refs[:g], refs[g:2 * g]
    (kn_ref, vn_ref, blast_ref, bnew_ref, farb_ref, o_ref,
     s_scr, kmt_scr, sel_scr, kn_scr, vn_scr, acc_scr, l_scr, qb_scr) = refs[2 * g:]
    p = pl.program_id(1)
    page = LANES
    pages_per_blk = MOBA_BLOCK // page
    n_blk = n_pages // pages_per_blk
    n_steps = n_pages // g
    lane = lax.broadcasted_iota(I32, (LANES, LANES), 1)

    @pl.when(p == 0)
    def _():
        qb_scr[...] = qbt_ref[0].astype(BF16)
        kmt_scr[...] = jnp.zeros(kmt_scr.shape, F32)

    @pl.when(p < n_steps)
    def _():
        lane_k = lax.broadcasted_iota(I32, kmt_scr.shape, 1)
        for bi in range(g // pages_per_blk):
            ks = [kt_refs[bi * pages_per_blk + u][0] for u in range(pages_per_blk)]
            s = jnp.dot(qb_scr[...], jnp.concatenate(ks, axis=1).astype(BF16), preferred_element_type=F32)
            for u in range(pages_per_blk):
                s_scr[p * g + bi * pages_per_blk + u] = s[:, u * page:(u + 1) * page]
            tot = ks[0]
            for u in range(1, pages_per_blk):
                tot = tot + ks[u]
            blk_sum = jnp.sum(tot, axis=1, keepdims=True)
            b = p * (g // pages_per_blk) + bi
            kmt_scr[...] = kmt_scr[...] + jnp.where(lane_k == b, blk_sum, 0.0)

    @pl.when(p == n_steps - 1)
    def _():
        gate = jnp.dot(qbt_ref[0], kmt_scr[...], precision=HIGHEST, preferred_element_type=F32) * (1.0 / MOBA_BLOCK)
        gate = jnp.where(lane < n_blk, gate, -jnp.inf)
        sel_scr[...] = _top3_lanes(gate, lane, LANES)

        kn_scr[...] = jnp.zeros(kn_scr.shape, F32)
        kn_scr[0:t_new, :] = kn_ref[0]
        vn_scr[...] = jnp.zeros(vn_scr.shape, F32)
        vn_scr[0:t_new, :] = vn_ref[0]
        sn = _dot_nt(qb_scr[...], kn_scr[...].astype(BF16)) + bnew_ref[...]
        qi = lax.rem(lax.broadcasted_iota(I32, (LANES, LANES), 0), t_new)
        sn = jnp.where((lane < t_new) & (lane <= qi), sn, NEG_INF)
        farb = farb_ref[...]

        def pass_max(pg, macc):
            keep = jnp.sum(jnp.where(lane == pg // pages_per_blk, sel_scr[...], 0.0), axis=1, keepdims=True) > 0.0
            bias = jnp.where(pg == n_pages - 1, blast_ref[...], farb)
            s = jnp.where(keep, s_scr[pg] + bias, NEG_INF)
            s_scr[pg] = s
            return jnp.maximum(macc, s)
        m = jnp.max(lax.fori_loop(0, n_pages, pass_max, sn), axis=1, keepdims=True)

        def pass_exp(pg, lacc):
            e = jnp.exp(s_scr[pg] - m)
            s_scr[pg] = e
            return lacc + e
        en = jnp.exp(sn - m)
        l = jnp.sum(lax.fori_loop(0, n_pages, pass_exp, en), axis=1, keepdims=True)
        l_scr[...] = jnp.broadcast_to(l, l_scr.shape)
        acc_scr[...] = jnp.dot(en.astype(BF16), vn_scr[...].astype(BF16), preferred_element_type=F32)

    @pl.when(p >= n_steps)
    def _():
        pg0 = (p - n_steps) * g
        acc = acc_scr[...]
        for bi in range(g // pages_per_blk):
            first = bi * pages_per_blk
            pc = jnp.concatenate([s_scr[pg0 + first + u] for u in range(pages_per_blk)], axis=1).astype(BF16)
            vc = jnp.concatenate([vt_refs[first + u][0] for u in range(pages_per_blk)], axis=1).astype(BF16)
            acc = acc + _dot_nt(pc, vc)
        acc_scr[...] = acc

    @pl.when(p == 2 * n_steps - 1)
    def _():
        acc = acc_scr[...] / l_scr[:, 0:1]
        rh = lax.broadcasted_iota(I32, acc.shape, 0) // t_new
        ch = lax.broadcasted_iota(I32, acc.shape, 1) // B_HEAD_DIM
        acc = jnp.where(rh == ch, acc, 0.0)
        o_ref[0] = jnp.sum(acc.reshape(B_HEADS, t_new, D_MODEL), axis=0)


def _moba_sample(q, kv, cache_kt, cache_vt, page_table, bias_tbl):
    s, t_new, d = q.shape
    n_pages = page_table.shape[1]
    page = cache_kt.shape[2]
    g = SAMPLE_PAGES_PER_STEP
    pages_per_blk = MOBA_BLOCK // page
    assert B_HEADS * t_new == LANES and page == LANES and n_pages % g == 0 and g % pages_per_blk == 0
    assert n_pages // pages_per_blk <= LANES
    n_steps = n_pages // g
    scale = B_HEAD_DIM ** -0.5
    qh = (q * scale).reshape(s, t_new, B_HEADS, B_HEAD_DIM).transpose(0, 2, 1, 3)
    eye = jnp.eye(B_HEADS, dtype=F32)
    qbt = (qh[:, :, :, None, :] * eye[None, :, None, :, None]).reshape(s, LANES, d)
    bias_last = jnp.stack([jnp.flip(bias_tbl[:, 1 + i:1 + i + page], axis=1) for i in range(t_new)], axis=1)
    bias_last = bias_last.reshape(LANES, page)
    near = np.maximum(np.arange(t_new)[:, None] - np.arange(t_new)[None, :], 0)
    bias_new = jnp.stack([jnp.stack([bias_tbl[:, near[i, j]] for j in range(t_new)], axis=1)
                          for i in range(t_new)], axis=1).reshape(LANES, t_new)
    bias_new = jnp.pad(bias_new, ((0, 0), (0, LANES - t_new)))
    far_bias = jnp.broadcast_to(jnp.repeat(bias_tbl[:, 2 * MOBA_BLOCK - 1], t_new)[:, None], (LANES, LANES))
    k_new = kv[..., :d]
    v_new = kv[..., d:]
    kmaps = [lambda b, p, pt, u=u: (pt[b * n_pages + g * jnp.minimum(p, n_steps - 1) + u], 0, 0)
             for u in range(g)]
    vmaps = [lambda b, p, pt, u=u: (pt[b * n_pages + g * jnp.maximum(p - n_steps, 0) + u], 0, 0)
             for u in range(g)]
    seq = lambda b, p, pt: (b, 0, 0)
    fixed = lambda b, p, pt: (0, 0)
    grid_spec = pltpu.PrefetchScalarGridSpec(
        num_scalar_prefetch=1,
        grid=(s, 2 * n_steps),
        in_specs=[pl.BlockSpec((1, LANES, d), seq)]
                 + [pl.BlockSpec((1, d, page), m) for m in kmaps]
                 + [pl.BlockSpec((1, d, page), m) for m in vmaps]
                 + [pl.BlockSpec((1, t_new, d), seq), pl.BlockSpec((1, t_new, d), seq),
                    pl.BlockSpec((LANES, page), fixed), pl.BlockSpec((LANES, LANES), fixed),
                    pl.BlockSpec((LANES, LANES), fixed)],
        out_specs=pl.BlockSpec((1, t_new, d), seq),
        scratch_shapes=[pltpu.VMEM((n_pages, LANES, page), F32), pltpu.VMEM((d, LANES), F32),
                        pltpu.VMEM((LANES, LANES), F32), pltpu.VMEM((LANES, d), F32),
                        pltpu.VMEM((LANES, d), F32), pltpu.VMEM((LANES, d), F32),
                        pltpu.VMEM((LANES, LANES), F32), pltpu.VMEM((LANES, d), BF16)],
    )
    return pl.pallas_call(
        functools.partial(_moba_s_body, n_pages, t_new),
        grid_spec=grid_spec,
        out_shape=jax.ShapeDtypeStruct((s, t_new, d), F32),
        compiler_params=_params(("arbitrary", "arbitrary")),
        name="moba_sample",
    )(page_table.reshape(-1).astype(I32), qbt, *([cache_kt] * g), *([cache_vt] * g),
      k_new, v_new, bias_last, bias_new, far_bias)


def kernel(x_prompt, x_sample, cache_k, cache_v, state_hgrn, page_table, a_w_in, a_lb, a_norm, a_w_out,
           kv_w, b_w_q, b_w_o, rel_bias, router_w, router_b, w_up, b_up, w_down, b_down, ln_g, ln_b):
    bp, tp, d = x_prompt.shape
    bs, ts, _ = x_sample.shape
    n_p = bp * tp
    n_s = bs * ts
    x = jnp.concatenate([x_prompt.reshape(n_p, d), x_sample.reshape(n_s, d)], axis=0)

    p_lb = jax.nn.softmax(a_lb.astype(F32), axis=0)
    lower = jnp.cumsum(p_lb, axis=0) - p_lb[0]
    bias_tbl = _bias_by_distance(rel_bias, 2 * MOBA_BLOCK)
    zero_state = jnp.zeros((bp,) + state_hgrn.shape[2:], F32)
    hgrn_chunk = 128 if tp % 128 == 0 else tp

    states_p, states_s = [], []
    kv = None
    for l in range(DEPTH):
        if l < N_A_LAYERS:
            proj = _proj(x, a_w_in[l].astype(BF16))
            o_p, s_p = _hgrn(proj, 0, bp, tp, zero_state, lower[l], a_norm[l], hgrn_chunk)
            o_s, s_s = _hgrn(proj, n_p, bs, ts, state_hgrn[l], lower[l], a_norm[l], ts)
            states_p.append(s_p)
            states_s.append(s_s)
            mix = jnp.concatenate([o_p, o_s], axis=0)
            x = _mm_res_ln(x, mix, a_w_out[l].astype(BF16), ln_g[l, 0], ln_b[l, 0])
        else:
            j = l - N_A_LAYERS
            if kv is None:
                kv, vt_all, km_all = _kv_proj(x, kv_w.astype(BF16))
                kv_p = kv[:n_p].reshape(bp, tp, 2 * d)
                kv_s = kv[n_p:].reshape(bs, ts, 2 * d)
                n_kb = tp // MOBA_BLOCK
                vt_p = vt_all[:bp * n_kb].reshape(bp, n_kb, d, MOBA_BLOCK)
                km_p = km_all[:bp * n_kb].reshape(bp, n_kb, d)
                n_phys, page = cache_k.shape[:2]
                cache_kt = cache_k.transpose(0, 2, 3, 1).reshape(n_phys, d, page)
                cache_vt = cache_v.transpose(0, 2, 3, 1).reshape(n_phys, d, page)
            q = _proj(x, b_w_q[j].astype(BF16))
            o_p = _moba_prompt(q, kv, vt_p, km_p, bias_tbl, bp, tp)
            o_s = _moba_sample(q[n_p:].reshape(bs, ts, d), kv_s, cache_kt, cache_vt, page_table, bias_tbl)
            mix = jnp.concatenate([o_p.reshape(n_p, d), o_s.reshape(n_s, d)], axis=0)
            x = _mm_res_ln(x, mix, b_w_o[j].astype(BF16), ln_g[l, 0], ln_b[l, 0])
        bg = b_up[l, :, 0::2].reshape(N_EXPERTS, 1, d)
        bl = b_up[l, :, 1::2].reshape(N_EXPERTS, 1, d)
        x = _moe(x, l, router_w[l], router_b[l], w_up, bg, bl, w_down,
                 b_down[l].reshape(N_EXPERTS, 1, d), ln_g[l, 1], ln_b[l, 1])

    y_p = x[:n_p].reshape(bp, tp, d)
    y_s = x[n_p:].reshape(bs, ts, d)
    heads = (B_HEADS, B_HEAD_DIM)
    return (y_p, y_s,
            kv_p[..., :d].reshape(bp, tp, *heads), kv_p[..., d:].reshape(bp, tp, *heads),
            jnp.stack(states_p),
            kv_s[..., :d].reshape(bs, ts, *heads), kv_s[..., d:].reshape(bs, ts, *heads),
            jnp.stack(states_s))
```
